```python
import jax
import jax.numpy as jnp
from jax import lax
import numpy as np

D_MODEL = 1024
BATCH = 2
SEQ = 16384
DEPTH = 2

HEAD_DIM = 64
ROPE_THETA = 10000.0
NORM_EPS = 1e-6

MOBA_HEADS = 8
MOBA_BLOCK = 256
MOBA_TOPK = 3
MOBA_Q_CHUNK = 128
MOBA_WIDTH = MOBA_HEADS * HEAD_DIM
RET_HEADS = 4
RET_QK_DIM = 64
RET_V_DIM = 128
RET_CHUNK = 128
RET_THETA = 10000.0
RET_QK_WIDTH = RET_HEADS * RET_QK_DIM
RET_V_WIDTH = RET_HEADS * RET_V_DIM
EVEN_IN_WIDTH = 3 * MOBA_WIDTH + 2 * RET_QK_WIDTH + 2 * RET_V_WIDTH
EVEN_MIX_WIDTH = MOBA_WIDTH + RET_V_WIDTH

DIL_HEADS = D_MODEL // HEAD_DIM
DIL_PATTERNS = ((128, 1), (512, 4), (2048, 16))
DIL_BLOCK = 128
ODD_IN_WIDTH = 3 * DIL_HEADS * HEAD_DIM
ODD_MIX_WIDTH = DIL_HEADS * HEAD_DIM

MOE_GROUPS = 4
MOE_EXPERTS_PER_GROUP = 8
MOE_EXPERTS = MOE_GROUPS * MOE_EXPERTS_PER_GROUP
MOE_TOPK = 2
MOE_HIDDEN = D_MODEL // 2
MOE_ROW_BLOCK = 128

kernel_name = 'hybrid_moba_retnet_dilated_hmoe'


def rms_norm(x, gain):
    xf = x.astype(jnp.float32)
    y = xf * lax.rsqrt(jnp.mean(xf * xf, axis=-1, keepdims=True) + NORM_EPS)
    return (y * gain.astype(jnp.float32)).astype(x.dtype)


def split_heads(t, n_heads):
    b, s, _ = t.shape
    return t.reshape(b, s, n_heads, -1).transpose(0, 2, 1, 3)


def merge_heads(t):
    b, h, s, d = t.shape
    return t.transpose(0, 2, 1, 3).reshape(b, s, h * d)


def rotary(x):
    s, dh = x.shape[2], x.shape[3]
    half = dh // 2
    inv_freq = 1.0 / (ROPE_THETA ** (jnp.arange(half, dtype=jnp.float32) * (2.0 / dh)))
    ang = jnp.arange(s, dtype=jnp.float32)[:, None] * inv_freq[None, :]
    cos, sin = jnp.cos(ang), jnp.sin(ang)
    xf = x.astype(jnp.float32)
    x1, x2 = xf[..., :half], xf[..., half:]
    return jnp.concatenate([x1 * cos - x2 * sin, x1 * sin + x2 * cos], axis=-1).astype(x.dtype)


def retnet_rotation(x):
    s, dk = x.shape[2], x.shape[3]
    half = dk // 2
    freq = 1.0 / (RET_THETA ** jnp.linspace(0.0, 1.0, half, dtype=jnp.float32))
    ang = jnp.arange(s, dtype=jnp.float32)[:, None] * freq[None, :]
    cos, sin = jnp.cos(ang), jnp.sin(ang)
    xf = x.astype(jnp.float32).reshape(x.shape[:-1] + (half, 2))
    xe, xo = xf[..., 0], xf[..., 1]
    out = jnp.stack([xe * cos - xo * sin, xe * sin + xo * cos], axis=-1)
    return out.reshape(x.shape).astype(x.dtype)


def moba_attention(q, k, v):
    b, h, s, dh = q.shape
    n_blk = -(-s // MOBA_BLOCK)
    sp = n_blk * MOBA_BLOCK
    pad = ((0, 0), (0, 0), (0, sp - s), (0, 0))
    q, k, v = jnp.pad(q, pad), jnp.pad(k, pad), jnp.pad(v, pad)
    scale = dh ** -0.5
    k_blocks = k.reshape(b, h, n_blk, MOBA_BLOCK, dh)
    v_blocks = v.reshape(b, h, n_blk, MOBA_BLOCK, dh)
    k_mean = k_blocks.astype(jnp.float32).mean(axis=3).astype(q.dtype)
    gate = jnp.einsum('bhsd,bhnd->bhsn', q, k_mean).astype(jnp.float32)
    q_blk = jnp.arange(sp) // MOBA_BLOCK
    fully_past = jnp.arange(n_blk)[None, :] < q_blk[:, None]
    gate = jnp.where(fully_past, gate, -jnp.inf)
    n_sel = min(MOBA_TOPK, n_blk)
    _, sel = lax.top_k(gate, n_sel)
    sel_ok = sel < q_blk[:, None]
    bi = jnp.arange(b)[:, None, None, None]
    hi = jnp.arange(h)[None, :, None, None]
    n_chunk = sp // MOBA_Q_CHUNK

    def chunk(c):
        q0 = c * MOBA_Q_CHUNK
        blk = q0 // MOBA_BLOCK
        qc = lax.dynamic_slice_in_dim(q, q0, MOBA_Q_CHUNK, axis=2)
        k_own = lax.dynamic_slice_in_dim(k, blk * MOBA_BLOCK, MOBA_BLOCK, axis=2)
        v_own = lax.dynamic_slice_in_dim(v, blk * MOBA_BLOCK, MOBA_BLOCK, axis=2)
        sel_c = lax.dynamic_slice_in_dim(sel, q0, MOBA_Q_CHUNK, axis=2)
        ok_c = lax.dynamic_slice_in_dim(sel_ok, q0, MOBA_Q_CHUNK, axis=2)
        k_sel = k_blocks[bi, hi, sel_c]
        v_sel = v_blocks[bi, hi, sel_c]
        qpos = q0 + jnp.arange(MOBA_Q_CHUNK)
        kpos = blk * MOBA_BLOCK + jnp.arange(MOBA_BLOCK)
        s_own = jnp.einsum('bhqd,bhkd->bhqk', qc, k_own).astype(jnp.float32) * scale
        s_own = jnp.where(kpos[None, :] <= qpos[:, None], s_own, -jnp.inf)
        s_sel = jnp.einsum('bhqd,bhqnkd->bhqnk', qc, k_sel).astype(jnp.float32) * scale
        s_sel = jnp.where(ok_c[..., None], s_sel, -jnp.inf)
        s_all = jnp.concatenate([s_own, s_sel.reshape(b, h, MOBA_Q_CHUNK, n_sel * MOBA_BLOCK)], axis=-1)
        p = jax.nn.softmax(s_all, axis=-1).astype(v.dtype)
        p_own = p[..., :MOBA_BLOCK]
        p_sel = p[..., MOBA_BLOCK:].reshape(b, h, MOBA_Q_CHUNK, n_sel, MOBA_BLOCK)
        return (jnp.einsum('bhqk,bhkd->bhqd', p_own, v_own)
                + jnp.einsum('bhqnk,bhqnkd->bhqd', p_sel, v_sel))

    out = lax.map(chunk, jnp.arange(n_chunk))
    return out.transpose(1, 2, 0, 3, 4).reshape(b, h, sp, dh)[:, :, :s]


def retention(q, k, v):
    b, h, s, dk = q.shape
    dv = v.shape[-1]
    c = RET_CHUNK
    n = s // c
    log_gamma = jnp.log(1.0 - 2.0 ** (-5.0 - jnp.arange(h, dtype=jnp.float32)))
    qf = q.astype(jnp.float32).reshape(b, h, n, c, dk)
    kf = k.astype(jnp.float32).reshape(b, h, n, c, dk) * (dk ** -0.5)
    vf = v.astype(jnp.float32).reshape(b, h, n, c, dv)
    pos = jnp.arange(c, dtype=jnp.float32)
    rel = pos[:, None] - pos[None, :]
    decay = jnp.where(rel >= 0, jnp.exp(log_gamma[:, None, None] * jnp.maximum(rel, 0.0)), 0.0)
    scores = jnp.einsum('bhnid,bhnjd->bhnij', qf, kf) * decay[None, :, None]
    intra = jnp.einsum('bhnij,bhnje->bhnie', scores, vf)
    k_tail = kf * jnp.exp(log_gamma[:, None] * (c - 1.0 - pos))[None, :, None, :, None]
    chunk_state = jnp.einsum('bhnjd,bhnje->nbhde', k_tail, vf)
    gamma_c = jnp.exp(log_gamma * c)[None, :, None, None]

    def step(r, u):
        return gamma_c * r + u, r

    _, r_prev = lax.scan(step, jnp.zeros((b, h, dk, dv), jnp.float32), chunk_state)
    q_head = qf * jnp.exp(log_gamma[:, None] * (pos + 1.0))[None, :, None, :, None]
    inter = jnp.einsum('bhnid,nbhde->bhnie', q_head, r_prev)
    y = (intra + inter).reshape(b, h, s, dv)
    mu = jnp.mean(y, axis=-1, keepdims=True)
    var = jnp.mean(jnp.square(y - mu), axis=-1, keepdims=True)
    return ((y - mu) * lax.rsqrt(var + NORM_EPS)).astype(v.dtype)


def dilated_branch(q, k, v, window, dil):
    b, h, s, dh = q.shape
    w_d = window // dil
    blk = DIL_BLOCK
    n_blk = -(-s // (dil * blk))
    length = n_blk * blk
    sp = length * dil
    pad = ((0, 0), (0, 0), (0, sp - s), (0, 0))
    q, k, v = jnp.pad(q, pad), jnp.pad(k, pad), jnp.pad(v, pad)

    def decimate(t):
        return t.reshape(b, h, length, dil, dh).transpose(0, 1, 3, 2, 4).reshape(b, h, dil, n_blk, blk, dh)

    def with_prev(t):
        prev = jnp.concatenate([jnp.zeros_like(t[:, :, :, :1]), t[:, :, :, :-1]], axis=3)
        return jnp.concatenate([prev, t], axis=4)

    qd = decimate(q)
    kc = with_prev(decimate(k))
    vc = with_prev(decimate(v))
    a = jnp.arange(blk)[:, None]
    e = jnp.arange(2 * blk)[None, :]
    dist = blk + a - e
    band = (dist >= 0) & (dist <= w_d)
    not_before_start = (jnp.arange(n_blk)[:, None, None] > 0) | (e >= blk)[None]
    mask = band[None] & not_before_start
    sc = jnp.einsum('bhrnqd,bhrnkd->bhrnqk', qd, kc).astype(jnp.float32) * (dh ** -0.5)
    sc = jnp.where(mask, sc, -jnp.inf)
    m = jnp.max(sc, axis=-1, keepdims=True)
    p = jnp.exp(sc - m)
    den = jnp.sum(p, axis=-1, keepdims=True)
    o = jnp.einsum('bhrnqk,bhrnkd->bhrnqd', (p / den).astype(v.dtype), vc)
    lse = (m + jnp.log(den))[..., 0]
    o = o.reshape(b, h, dil, length, dh).transpose(0, 1, 3, 2, 4).reshape(b, h, sp, dh)[:, :, :s]
    lse = lse.reshape(b, h, dil, length).transpose(0, 1, 3, 2).reshape(b, h, sp)[:, :, :s]
    return o, lse


def dilated_mixture(q, k, v):
    branches = [dilated_branch(q, k, v, w, d) for (w, d) in DIL_PATTERNS]
    lses = jnp.stack([l for (_, l) in branches], axis=0)
    wts = jax.nn.softmax(lses, axis=0)
    out = wts[0][..., None] * branches[0][0].astype(jnp.float32)
    for i in range(1, len(branches)):
        out = out + wts[i][..., None] * branches[i][0].astype(jnp.float32)
    return out.astype(q.dtype)


def even_mixer(h, w_in, w_out):
    proj = h @ w_in
    s1 = MOBA_WIDTH
    s2 = 2 * MOBA_WIDTH
    s3 = 3 * MOBA_WIDTH
    s4 = s3 + RET_QK_WIDTH
    s5 = s4 + RET_QK_WIDTH
    s6 = s5 + RET_V_WIDTH
    aq, ak, av, rq, rk, rv, rg = jnp.split(proj, [s1, s2, s3, s4, s5, s6], axis=-1)
    a_out = moba_attention(rotary(split_heads(aq, MOBA_HEADS)),
                           rotary(split_heads(ak, MOBA_HEADS)),
                           split_heads(av, MOBA_HEADS))
    r_out = retention(retnet_rotation(split_heads(rq, RET_HEADS)),
                      retnet_rotation(split_heads(rk, RET_HEADS)),
                      split_heads(rv, RET_HEADS))
    r_out = merge_heads(r_out) * jax.nn.silu(rg)
    return jnp.concatenate([merge_heads(a_out), r_out], axis=-1) @ w_out


def odd_mixer(h, w_in, w_out):
    q, k, v = jnp.split(h @ w_in, 3, axis=-1)
    o = dilated_mixture(rotary(split_heads(q, DIL_HEADS)),
                        rotary(split_heads(k, DIL_HEADS)),
                        split_heads(v, DIL_HEADS))
    return merge_heads(o) @ w_out


def hier_moe(h, w_group, b_group, w_fine, b_fine, w_gate, w_up, w_down):
    b, s, d = h.shape
    n_tok = b * s
    x2 = h.reshape(n_tok, d)
    g_logits = (x2 @ w_group).astype(jnp.float32) + b_group.astype(jnp.float32)
    g_prob = jax.nn.softmax(g_logits, axis=-1)
    g_val, grp = lax.top_k(g_prob, 1)
    f_logits = ((x2 @ w_fine).astype(jnp.float32) + b_fine.astype(jnp.float32)).reshape(
        n_tok, MOE_GROUPS, MOE_EXPERTS_PER_GROUP)
    f_in_group = jnp.take_along_axis(f_logits, grp[:, :, None], axis=1)[:, 0]
    top_v, top_i = lax.top_k(f_in_group, MOE_TOPK)
    gates = g_val * jax.nn.softmax(top_v, axis=-1)
    expert = grp * MOE_EXPERTS_PER_GROUP + top_i
    n_assign = n_tok * MOE_TOPK
    e_flat = expert.reshape(-1)
    t_flat = jnp.repeat(jnp.arange(n_tok, dtype=jnp.int32), MOE_TOPK)
    g_flat = gates.reshape(-1)
    order = jnp.argsort(e_flat)
    e_s, t_s, g_s = e_flat[order], t_flat[order], g_flat[order]
    counts = jnp.bincount(e_flat, length=MOE_EXPERTS)
    starts = jnp.cumsum(counts) - counts
    padded = ((counts + MOE_ROW_BLOCK - 1) // MOE_ROW_BLOCK) * MOE_ROW_BLOCK
    p_ends = jnp.cumsum(padded)
    p_starts = p_ends - padded
    dest = p_starts[e_s] + (jnp.arange(n_assign) - starts[e_s])
    n_blocks = -(-n_assign // MOE_ROW_BLOCK) + MOE_EXPERTS
    n_rows = n_blocks * MOE_ROW_BLOCK
    row_tok = jnp.zeros((n_rows,), jnp.int32).at[dest].set(t_s)
    row_gate = jnp.zeros((n_rows,), jnp.float32).at[dest].set(g_s)
    blk_start = jnp.arange(n_blocks) * MOE_ROW_BLOCK
    blk_exp = jnp.minimum(jnp.searchsorted(p_ends, blk_start, side='right'), MOE_EXPERTS - 1)

    def expert_block(args):
        tok, e = args
        xb = x2[tok]
        hid = jax.nn.silu(xb @ w_gate[e]) * (xb @ w_up[e])
        return hid @ w_down[e]

    ys = lax.map(expert_block, (row_tok.reshape(n_blocks, MOE_ROW_BLOCK), blk_exp)).reshape(n_rows, d)
    out = jnp.zeros((n_tok, d), jnp.float32).at[row_tok].add(ys.astype(jnp.float32) * row_gate[:, None])
    return out.astype(h.dtype).reshape(b, s, d)


def setup_inputs(seed: int = 0) -> dict:
    key = jax.random.key(seed)
    ks = jax.random.split(key, 17)
    n_even = (DEPTH + 1) // 2
    n_odd = DEPTH // 2

    def dense(k, shape, fan_in):
        return jax.random.normal(k, shape, jnp.float32) * (fan_in ** -0.5)

    def gain(k, shape):
        return 1.0 + 0.02 * jax.random.normal(k, shape, jnp.float32)

    return {
        'x': jax.random.normal(ks[0], (BATCH, SEQ, D_MODEL), jnp.float32),
        'mix_norm_even': gain(ks[1], (n_even, D_MODEL)),
        'w_in_even': dense(ks[2], (n_even, D_MODEL, EVEN_IN_WIDTH), D_MODEL),
        'w_out_even': dense(ks[3], (n_even, EVEN_MIX_WIDTH, D_MODEL), EVEN_MIX_WIDTH),
        'mix_norm_odd': gain(ks[4], (n_odd, D_MODEL)),
        'w_in_odd': dense(ks[5], (n_odd, D_MODEL, ODD_IN_WIDTH), D_MODEL),
        'w_out_odd': dense(ks[6], (n_odd, ODD_MIX_WIDTH, D_MODEL), ODD_MIX_WIDTH),
        'ffn_norm': gain(ks[7], (DEPTH, D_MODEL)),
        'w_router_group': dense(ks[8], (DEPTH, D_MODEL, MOE_GROUPS), D_MODEL),
        'b_router_group': 0.01 * jax.random.normal(ks[9], (DEPTH, MOE_GROUPS), jnp.float32),
        'w_router_expert': dense(ks[10], (DEPTH, D_MODEL, MOE_EXPERTS), D_MODEL),
        'b_router_expert': 0.01 * jax.random.normal(ks[11], (DEPTH, MOE_EXPERTS), jnp.float32),
        'w_expert_gate': dense(ks[12], (DEPTH, MOE_EXPERTS, D_MODEL, MOE_HIDDEN), D_MODEL),
        'w_expert_up': dense(ks[13], (DEPTH, MOE_EXPERTS, D_MODEL, MOE_HIDDEN), D_MODEL),
        'w_expert_down': dense(ks[14], (DEPTH, MOE_EXPERTS, MOE_HIDDEN, D_MODEL), MOE_HIDDEN),
        'final_norm': gain(ks[15], (D_MODEL,)),
    }


def reference(x, mix_norm_even, w_in_even, w_out_even, mix_norm_odd, w_in_odd, w_out_odd,
              ffn_norm, w_router_group, b_router_group, w_router_expert, b_router_expert,
              w_expert_gate, w_expert_up, w_expert_down, final_norm):
    h = x
    for layer in range(DEPTH):
        i = layer // 2
        if layer % 2 == 0:
            h = h + even_mixer(rms_norm(h, mix_norm_even[i]), w_in_even[i], w_out_even[i])
        else:
            h = h + odd_mixer(rms_norm(h, mix_norm_odd[i]), w_in_odd[i], w_out_odd[i])
        h = h + hier_moe(rms_norm(h, ffn_norm[layer]), w_router_group[layer], b_router_group[layer],
                         w_router_expert[layer], b_router_expert[layer], w_expert_gate[layer],
                         w_expert_up[layer], w_expert_down[layer])
    return rms_norm(h, final_norm)
```

```python
import functools

import jax
import jax.numpy as jnp
from jax import lax
from jax.experimental import pallas as pl
from jax.experimental.pallas import tpu as pltpu

F32 = jnp.float32
BF16 = jnp.bfloat16

D_MODEL = 1024
HEAD_DIM = 64
ROPE_THETA = 10000.0
RET_THETA = 10000.0
NORM_EPS = 1e-6

MOBA_HEADS = 8
MOBA_BLOCK = 256
MOBA_TOPK = 3
MOBA_WIDTH = MOBA_HEADS * HEAD_DIM
RET_HEADS = 4
RET_QK_DIM = 64
RET_V_DIM = 128
RET_CHUNK = 128
RET_QK_WIDTH = RET_HEADS * RET_QK_DIM
RET_V_WIDTH = RET_HEADS * RET_V_DIM
EVEN_IN_WIDTH = 3 * MOBA_WIDTH + 2 * RET_QK_WIDTH + 2 * RET_V_WIDTH
EVEN_QKV_WIDTH = EVEN_IN_WIDTH - RET_V_WIDTH

DIL_HEADS = D_MODEL // HEAD_DIM
DIL_PATTERNS = ((128, 1), (512, 4), (2048, 16))
DIL_BLOCK = 128
ODD_IN_WIDTH = 3 * D_MODEL

MOE_GROUPS = 4
MOE_EXPERTS_PER_GROUP = 8
MOE_EXPERTS = MOE_GROUPS * MOE_EXPERTS_PER_GROUP
MOE_TOPK = 2
MOE_HIDDEN = D_MODEL // 2

LANES = 128
ROUTER_WIDTH = LANES
PROJ_ROWS = 512
EXPERT_ROWS = 256
DIL_ROWS = 512
RET_ROWS = 512
VMEM_LIMIT = 56 * 1024 * 1024
NEG_INF = float("-inf")


def _dot(a, b):
    return jnp.dot(a, b, preferred_element_type=F32)


def _dot_nt(a, b):
    return lax.dot_general(a, b, (((1,), (1,)), ((), ())), preferred_element_type=F32)


def _dot_tn(a, b):
    return lax.dot_general(a, b, (((0,), (0,)), ((), ())), preferred_element_type=F32)


def _rms(x, gain):
    ms = jnp.mean(x * x, axis=-1, keepdims=True)
    return x * lax.rsqrt(ms + NORM_EPS) * gain


def _swap_half(a, first_half):
    return jnp.where(first_half, pltpu.roll(a, LANES - 32, 1), pltpu.roll(a, 32, 1))


def _swap_pairs(a, even):
    return jnp.where(even, pltpu.roll(a, LANES - 1, 1), pltpu.roll(a, 1, 1))


def _rotary_tables(s):
    half = HEAD_DIM // 2
    inv_freq = 1.0 / (ROPE_THETA ** (jnp.arange(half, dtype=F32) * (2.0 / HEAD_DIM)))
    ang = jnp.arange(s, dtype=F32)[:, None] * inv_freq[None, :]
    cos, sin = jnp.cos(ang), jnp.sin(ang)
    return (jnp.concatenate([cos, cos, cos, cos], axis=-1),
            jnp.concatenate([-sin, sin, -sin, sin], axis=-1))


def _retnet_tables(s):
    half = RET_QK_DIM // 2
    freq = 1.0 / (RET_THETA ** jnp.linspace(0.0, 1.0, half, dtype=F32))
    ang = jnp.arange(s, dtype=F32)[:, None] * freq[None, :]
    cos, sin = jnp.cos(ang), jnp.sin(ang)
    cos2 = jnp.repeat(cos, 2, axis=-1)
    sin2 = jnp.stack([-sin, sin], axis=-1).reshape(s, RET_QK_DIM)
    return jnp.tile(cos2, (1, 2)), jnp.tile(sin2, (1, 2))


def _retention_tables():
    c = RET_CHUNK
    log_gamma = jnp.log(1.0 - 2.0 ** (-5.0 - jnp.arange(RET_HEADS, dtype=F32)))
    pos = jnp.arange(c, dtype=F32)
    rel = pos[:, None] - pos[None, :]
    decay = jnp.where(rel >= 0, jnp.exp(log_gamma[:, None, None] * jnp.maximum(rel, 0.0)), 0.0)
    tail = jnp.exp(log_gamma[:, None] * (c - 1.0 - pos))[:, :, None]
    head = jnp.exp(log_gamma[:, None] * (pos + 1.0))[:, :, None]
    gamma_c = jnp.exp(log_gamma * c)[:, None, None]
    return decay, tail, head, gamma_c


def _proj_even_kernel(x_ref, g_ref, w_ref, rc_ref, rs_ref, tc_ref, ts_ref,
                      qkv_ref, rg_ref, km_ref):
    tm = x_ref.shape[0]
    xn = _rms(x_ref[...], g_ref[...]).astype(BF16)
    lane = lax.broadcasted_iota(jnp.int32, (tm, LANES), 1)
    first_half = (lane % HEAD_DIM) < (HEAD_DIM // 2)
    even = (lane % 2) == 0
    rc, rs, tc, ts = rc_ref[...], rs_ref[...], tc_ref[...], ts_ref[...]
    scale = HEAD_DIM ** -0.5
    n_chunk = 4
    for j in range(EVEN_IN_WIDTH // (n_chunk * LANES)):
        acc = _dot(xn, w_ref[:, j * n_chunk * LANES:(j + 1) * n_chunk * LANES])
        for u in range(n_chunk):
            c = j * n_chunk + u
            a = acc[:, u * LANES:(u + 1) * LANES]
            cols = slice(c * LANES, (c + 1) * LANES)
            if c < 4:
                a = (a * rc + _swap_half(a, first_half) * rs) * scale
            elif c < 8:
                a = a * rc + _swap_half(a, first_half) * rs
                km_ref[0, :, (c - 4) * LANES:(c - 3) * LANES] = jnp.mean(
                    a.reshape(tm // MOBA_BLOCK, MOBA_BLOCK, LANES), axis=1)
            elif c < 12:
                pass
            elif c < 14:
                a = a * tc + _swap_pairs(a, even) * ts
            elif c < 16:
                a = (a * tc + _swap_pairs(a, even) * ts) * (RET_QK_DIM ** -0.5)
            elif c < 20:
                pass
            else:
                rg_ref[:, (c - 20) * LANES:(c - 19) * LANES] = a
                continue
            qkv_ref[:, cols] = a.astype(BF16)


def _proj_even(x2, gain, w_bf, seq):
    n = x2.shape[0]
    tm = PROJ_ROWS
    rc, rs = _rotary_tables(seq)
    tc, ts = _retnet_tables(seq)
    pos_blocks = seq // tm
    tab = pl.BlockSpec((tm, LANES), lambda i: (i % pos_blocks, 0))
    return pl.pallas_call(
        _proj_even_kernel,
        grid=(n // tm,),
        in_specs=[
            pl.BlockSpec((tm, D_MODEL), lambda i: (i, 0)),
            pl.BlockSpec((1, D_MODEL), lambda i: (0, 0)),
            pl.BlockSpec((D_MODEL, EVEN_IN_WIDTH), lambda i: (0, 0)),
            tab, tab, tab, tab,
        ],
        out_specs=[
            pl.BlockSpec((tm, EVEN_QKV_WIDTH), lambda i: (i, 0)),
            pl.BlockSpec((tm, RET_V_WIDTH), lambda i: (i, 0)),
            pl.BlockSpec((1, tm // MOBA_BLOCK, MOBA_WIDTH), lambda i: (i, 0, 0)),
        ],
        out_shape=[
            jax.ShapeDtypeStruct((n, EVEN_QKV_WIDTH), BF16),
            jax.ShapeDtypeStruct((n, RET_V_WIDTH), F32),
            jax.ShapeDtypeStruct((n // tm, tm // MOBA_BLOCK, MOBA_WIDTH), F32),
        ],
        compiler_params=pltpu.CompilerParams(
            dimension_semantics=("arbitrary",), vmem_limit_bytes=VMEM_LIMIT),
        name="proj_even",
    )(x2, gain.reshape(1, D_MODEL), w_bf, rc, rs, tc, ts)


def _moba_kernel(q_ref, k_ref, v_ref, km_ref, o_ref):
    i = pl.program_id(2)
    nblk = km_ref.shape[0]
    bq = MOBA_BLOCK
    blk = lax.broadcasted_iota(jnp.int32, (bq, nblk), 1)
    row = lax.broadcasted_iota(jnp.int32, (bq, bq), 0)
    col = lax.broadcasted_iota(jnp.int32, (bq, bq), 1)
    outs = []
    for h in range(2):
        sl = slice(h * HEAD_DIM, (h + 1) * HEAD_DIM)
        q = q_ref[:, sl]
        km = km_ref[:, sl]
        km_hi = km.astype(BF16)
        km_lo = (km - km_hi.astype(F32)).astype(BF16)
        gate = _dot_nt(q, km_hi) + _dot_nt(q, km_lo)
        g = jnp.where(blk < i, gate, NEG_INF)
        selm = jnp.zeros((bq, nblk), F32)
        for _ in range(MOBA_TOPK):
            m = jnp.max(g, axis=-1, keepdims=True)
            cand = (g == m) & (m > NEG_INF)
            idx = jnp.min(jnp.where(cand, blk, nblk), axis=-1, keepdims=True)
            pick = blk == idx
            selm = jnp.where(pick, 1.0, selm)
            g = jnp.where(pick, NEG_INF, g)

        own = pl.ds(pl.multiple_of(i * bq, bq), bq)
        s = jnp.where(col <= row, _dot_nt(q, k_ref[own, sl]), NEG_INF)
        m0 = jnp.max(s, axis=-1, keepdims=True)
        p = jnp.exp(s - m0)
        l0 = jnp.sum(p, axis=-1, keepdims=True)
        acc0 = _dot(p.astype(BF16), v_ref[own, sl])

        def body(j, carry, q=q, selm=selm, sl=sl):
            m, l, acc = carry
            chosen = jnp.sum(jnp.where(blk == j, selm, 0.0), axis=-1, keepdims=True) > 0.0
            rows = pl.ds(pl.multiple_of(j * bq, bq), bq)
            s = jnp.where(chosen, _dot_nt(q, k_ref[rows, sl]), NEG_INF)
            m_new = jnp.maximum(m, jnp.max(s, axis=-1, keepdims=True))
            alpha = jnp.exp(m - m_new)
            p = jnp.exp(s - m_new)
            l = alpha * l + jnp.sum(p, axis=-1, keepdims=True)
            acc = alpha * acc + _dot(p.astype(BF16), v_ref[rows, sl])
            return m_new, l, acc

        _, l, acc = lax.fori_loop(0, i, body, (m0, l0, acc0))
        outs.append(acc / l)
    o_ref[...] = jnp.concatenate(outs, axis=-1).astype(BF16)


def _moba(qkv3, kmean3):
    b, s, _ = qkv3.shape
    nblk = s // MOBA_BLOCK
    pairs = MOBA_HEADS // 2
    return pl.pallas_call(
        _moba_kernel,
        grid=(b, pairs, nblk),
        in_specs=[
            pl.BlockSpec((None, MOBA_BLOCK, LANES), lambda bi, hp, i: (bi, i, hp)),
            pl.BlockSpec((None, s, LANES), lambda bi, hp, i: (bi, 0, pairs + hp)),
            pl.BlockSpec((None, s, LANES), lambda bi, hp, i: (bi, 0, 2 * pairs + hp)),
            pl.BlockSpec((None, nblk, LANES), lambda bi, hp, i: (bi, 0, hp)),
        ],
        out_specs=pl.BlockSpec((None, MOBA_BLOCK, LANES), lambda bi, hp, i: (bi, i, hp)),
        out_shape=jax.ShapeDtypeStruct((b, s, MOBA_WIDTH), BF16),
        compiler_params=pltpu.CompilerParams(
            dimension_semantics=("arbitrary", "arbitrary", "arbitrary"),
            vmem_limit_bytes=VMEM_LIMIT),
        name="moba",
    )(qkv3, qkv3, qkv3, kmean3)


def _retention_kernel(q_ref, k_ref, v_ref, g_ref, dec_ref, tail_ref, head_ref, gc_ref,
                      o_ref, state_ref):
    @pl.when(pl.program_id(2) == 0)
    def _():
        state_ref[...] = jnp.zeros_like(state_ref)

    c = RET_CHUNK
    for h in range(2):
        qk = slice(h * RET_QK_DIM, (h + 1) * RET_QK_DIM)
        vv = slice(h * RET_V_DIM, (h + 1) * RET_V_DIM)
        dec, tail, head, gc = dec_ref[h], tail_ref[h], head_ref[h], gc_ref[h]
        for u in range(q_ref.shape[0] // c):
            rows = slice(u * c, (u + 1) * c)
            q, k, v = q_ref[rows, qk], k_ref[rows, qk], v_ref[rows, vv]
            scores = _dot_nt(q, k) * dec
            state = state_ref[h]
            y = (_dot(scores.astype(BF16), v)
                 + _dot((q.astype(F32) * head).astype(BF16), state.astype(BF16)))
            state_ref[h] = gc * state + _dot_tn((k.astype(F32) * tail).astype(BF16), v)
            mu = jnp.mean(y, axis=-1, keepdims=True)
            yc = y - mu
            var = jnp.mean(yc * yc, axis=-1, keepdims=True)
            gate = g_ref[rows, vv]
            swish = gate / (1.0 + jnp.exp(-gate))
            o_ref[rows, vv] = (yc * lax.rsqrt(var + NORM_EPS) * swish).astype(BF16)


def _retention(qkv3, rg3):
    b, s, _ = qkv3.shape
    t = RET_ROWS
    pairs = RET_HEADS // 2
    dec, tail, head, gc = _retention_tables()
    q0 = 3 * MOBA_WIDTH // LANES
    k0 = q0 + RET_QK_WIDTH // LANES
    v0 = (3 * MOBA_WIDTH + 2 * RET_QK_WIDTH) // (2 * RET_V_DIM)
    return pl.pallas_call(
        _retention_kernel,
        grid=(b, pairs, s // t),
        in_specs=[
            pl.BlockSpec((None, t, LANES), lambda bi, hp, n: (bi, n, q0 + hp)),
            pl.BlockSpec((None, t, LANES), lambda bi, hp, n: (bi, n, k0 + hp)),
            pl.BlockSpec((None, t, 2 * RET_V_DIM), lambda bi, hp, n: (bi, n, v0 + hp)),
            pl.BlockSpec((None, t, 2 * RET_V_DIM), lambda bi, hp, n: (bi, n, hp)),
            pl.BlockSpec((2, RET_CHUNK, RET_CHUNK), lambda bi, hp, n: (hp, 0, 0)),
            pl.BlockSpec((2, RET_CHUNK, 1), lambda bi, hp, n: (hp, 0, 0)),
            pl.BlockSpec((2, RET_CHUNK, 1), lambda bi, hp, n: (hp, 0, 0)),
            pl.BlockSpec((2, 1, 1), lambda bi, hp, n: (hp, 0, 0)),
        ],
        out_specs=pl.BlockSpec((None, t, 2 * RET_V_DIM), lambda bi, hp, n: (bi, n, hp)),
        out_shape=jax.ShapeDtypeStruct((b, s, RET_V_WIDTH), BF16),
        scratch_shapes=[pltpu.VMEM((2, RET_QK_DIM, RET_V_DIM), F32)],
        compiler_params=pltpu.CompilerParams(
            dimension_semantics=("arbitrary", "arbitrary", "arbitrary"),
            vmem_limit_bytes=VMEM_LIMIT),
        name="retention",
    )(qkv3, qkv3, qkv3, rg3, dec, tail, head, gc)


def _router_logits(xn, wr_hi_ref, wr_lo_ref):
    x_hi = xn.astype(BF16)
    x_lo = (xn - x_hi.astype(F32)).astype(BF16)
    w_hi = wr_hi_ref[...]
    return _dot(x_hi, w_hi) + _dot(x_lo, w_hi) + _dot(x_hi, wr_lo_ref[...])


def _outproj_even_kernel(x_ref, a_ref, r_ref, wo_ref, gn_ref, wr_hi_ref, wr_lo_ref,
                         h_ref, xn_ref, lg_ref):
    h = x_ref[...] + (_dot(a_ref[...], wo_ref[:MOBA_WIDTH, :])
                      + _dot(r_ref[...], wo_ref[MOBA_WIDTH:, :]))
    h_ref[...] = h
    xn = _rms(h, gn_ref[...])
    xn_ref[...] = xn
    lg_ref[...] = _router_logits(xn, wr_hi_ref, wr_lo_ref)


def _outproj_even(x2, a2, r2, wo_bf, gain, wr_hi, wr_lo):
    n = x2.shape[0]
    tm = PROJ_ROWS
    row = lambda w: pl.BlockSpec((tm, w), lambda i: (i, 0))
    full = lambda r, w: pl.BlockSpec((r, w), lambda i: (0, 0))
    return pl.pallas_call(
        _outproj_even_kernel,
        grid=(n // tm,),
        in_specs=[row(D_MODEL), row(MOBA_WIDTH), row(RET_V_WIDTH), full(D_MODEL, D_MODEL),
                  full(1, D_MODEL), full(D_MODEL, ROUTER_WIDTH), full(D_MODEL, ROUTER_WIDTH)],
        out_specs=[row(D_MODEL), row(D_MODEL), row(ROUTER_WIDTH)],
        out_shape=[jax.ShapeDtypeStruct((n, D_MODEL), F32),
                   jax.ShapeDtypeStruct((n, D_MODEL), F32),
                   jax.ShapeDtypeStruct((n, ROUTER_WIDTH), F32)],
        compiler_params=pltpu.CompilerParams(
            dimension_semantics=("arbitrary",), vmem_limit_bytes=VMEM_LIMIT),
        name="outproj_even",
    )(x2, a2, r2, wo_bf, gain.reshape(1, D_MODEL), wr_hi, wr_lo)


def _outproj_odd_kernel(h_ref, o1_ref, o2_ref, o3_ref, l1_ref, l2_ref, l3_ref, wo_ref,
                        gn_ref, wr_hi_ref, wr_lo_ref, hout_ref, xn_ref, lg_ref):
    pieces = []
    for hp in range(DIL_HEADS // 2):
        la, lb, lc = l1_ref[hp], l2_ref[hp], l3_ref[hp]
        m = jnp.maximum(jnp.maximum(la, lb), lc)
        ea, eb, ec = jnp.exp(la - m), jnp.exp(lb - m), jnp.exp(lc - m)
        mix = (ea * o1_ref[hp] + eb * o2_ref[hp] + ec * o3_ref[hp]) / (ea + eb + ec)
        pieces.append(mix.astype(BF16))
    h = h_ref[...] + _dot(jnp.concatenate(pieces, axis=-1), wo_ref[...])
    hout_ref[...] = h
    xn = _rms(h, gn_ref[...])
    xn_ref[...] = xn
    lg_ref[...] = _router_logits(xn, wr_hi_ref, wr_lo_ref)


def _outproj_odd(h2, branches, wo_bf, gain, wr_hi, wr_lo, seq):
    n = h2.shape[0]
    tm = PROJ_ROWS
    pairs = DIL_HEADS // 2
    per_seq = seq // tm
    row = lambda w: pl.BlockSpec((tm, w), lambda i: (i, 0))
    full = lambda r, w: pl.BlockSpec((r, w), lambda i: (0, 0))
    hm = pl.BlockSpec((None, pairs, tm, LANES), lambda i: (i // per_seq, 0, i % per_seq, 0))
    outs = [o for (o, _) in branches]
    lses = [l for (_, l) in branches]
    return pl.pallas_call(
        _outproj_odd_kernel,
        grid=(n // tm,),
        in_specs=[row(D_MODEL), hm, hm, hm, hm, hm, hm, full(D_MODEL, D_MODEL),
                  full(1, D_MODEL), full(D_MODEL, ROUTER_WIDTH), full(D_MODEL, ROUTER_WIDTH)],
        out_specs=[row(D_MODEL), row(D_MODEL), row(ROUTER_WIDTH)],
        out_shape=[jax.ShapeDtypeStruct((n, D_MODEL), F32),
                   jax.ShapeDtypeStruct((n, D_MODEL), F32),
                   jax.ShapeDtypeStruct((n, ROUTER_WIDTH), F32)],
        compiler_params=pltpu.CompilerParams(
            dimension_semantics=("arbitrary",), vmem_limit_bytes=VMEM_LIMIT),
        name="outproj_odd",
    )(h2, *outs, *lses, wo_bf, gain.reshape(1, D_MODEL), wr_hi, wr_lo)


def _route(logits, b_group, b_fine):
    n = logits.shape[0]
    g_logits = logits[:, :MOE_GROUPS] + b_group.astype(F32)
    g_prob = jax.nn.softmax(g_logits, axis=-1)
    g_val, grp = lax.top_k(g_prob, 1)
    f_logits = (logits[:, MOE_GROUPS:MOE_GROUPS + MOE_EXPERTS] + b_fine.astype(F32)).reshape(
        n, MOE_GROUPS, MOE_EXPERTS_PER_GROUP)
    f_in_group = jnp.take_along_axis(f_logits, grp[:, :, None], axis=1)[:, 0]
    top_v, top_i = lax.top_k(f_in_group, MOE_TOPK)
    gates = g_val * jax.nn.softmax(top_v, axis=-1)
    expert = grp * MOE_EXPERTS_PER_GROUP + top_i
    return expert.astype(jnp.int32), gates


def _plan_rows(expert, gates):
    rb = EXPERT_ROWS
    n_assign = expert.size
    e_flat = expert.reshape(-1)
    g_flat = gates.reshape(-1)
    order = jnp.argsort(e_flat).astype(jnp.int32)
    e_s = e_flat[order]
    counts = jnp.bincount(e_flat, length=MOE_EXPERTS).astype(jnp.int32)
    starts = jnp.cumsum(counts) - counts
    padded = ((counts + rb - 1) // rb) * rb
    p_ends = jnp.cumsum(padded)
    p_starts = p_ends - padded
    dest = p_starts[e_s] + (jnp.arange(n_assign, dtype=jnp.int32) - starts[e_s])
    n_blocks = n_assign // rb + MOE_EXPERTS
    n_rows = n_blocks * rb
    row_dst = jnp.zeros((n_rows,), jnp.int32).at[dest].set(order)
    row_gate = jnp.zeros((n_rows,), F32).at[dest].set(g_flat[order])
    blk_start = jnp.arange(n_blocks, dtype=jnp.int32) * rb
    blk_exp = jnp.minimum(jnp.searchsorted(p_ends, blk_start, side='right'),
                          MOE_EXPERTS - 1).astype(jnp.int32)
    blk_valid = jnp.clip(counts[blk_exp] - (blk_start - p_starts[blk_exp]), 0, rb).astype(jnp.int32)
    n_used = (p_ends[-1] // rb).astype(jnp.int32).reshape(1)
    return blk_exp, blk_valid, row_dst, n_used, row_gate.reshape(n_blocks, rb, 1)


def _expert_kernel(blk_exp, blk_valid, row_dst, n_used,
                   gate_ref, wg_ref, wu_ref, wd_ref, x_hbm, out_hbm,
                   xbuf, ybuf, gsem, ssem):
    del blk_exp
    rb = EXPERT_ROWS
    i = pl.program_id(0)
    used = n_used[0]
    slot = i % 2

    def gather_copy(tok, r, s):
        return pltpu.make_async_copy(x_hbm.at[pl.ds(tok, 1)], xbuf.at[s, pl.ds(r, 1)], gsem.at[s])

    def scatter_copy(dst, r, s):
        return pltpu.make_async_copy(ybuf.at[s, pl.ds(r, 1)], out_hbm.at[pl.ds(dst, 1)], ssem.at[s])

    def start_gather(b, s):
        def body(r, carry):
            gather_copy(row_dst[b * rb + r] >> 1, r, s).start()
            return carry
        lax.fori_loop(0, rb, body, 0)

    def wait_scatter(count, s):
        def body(r, carry):
            scatter_copy(0, 0, s).wait()
            return carry
        lax.fori_loop(0, count, body, 0)

    @pl.when(i == 0)
    def _():
        start_gather(0, 0)

    @pl.when(i < used)
    def _():
        @pl.when(i + 1 < used)
        def _():
            start_gather(i + 1, 1 - slot)

        def wait_body(r, carry):
            gather_copy(0, 0, slot).wait()
            return carry
        lax.fori_loop(0, rb, wait_body, 0)

        @pl.when(i >= 2)
        def _():
            wait_scatter(blk_valid[i - 2], slot)

        xb = xbuf[slot].astype(BF16)
        g = _dot(xb, wg_ref[...])
        u = _dot(xb, wu_ref[...])
        hid = (g / (1.0 + jnp.exp(-g)) * u).astype(BF16)
        ybuf[slot] = _dot(hid, wd_ref[...]) * gate_ref[...]

        def scatter_body(r, carry):
            scatter_copy(row_dst[i * rb + r], r, slot).start()
            return carry
        lax.fori_loop(0, blk_valid[i], scatter_body, 0)

        @pl.when(i == used - 1)
        def _():
            wait_scatter(blk_valid[i], slot)

            @pl.when(i >= 1)
            def _():
                wait_scatter(blk_valid[i - 1], 1 - slot)


def _experts(xn, plan, wg_bf, wu_bf, wd_bf):
    blk_exp, blk_valid, row_dst, n_used, row_gate = plan
    n = xn.shape[0]
    rb = EXPERT_ROWS
    n_blocks = blk_exp.shape[0]
    w_in = pl.BlockSpec((None, D_MODEL, MOE_HIDDEN), lambda i, be, bv, rd, nu: (be[i], 0, 0))
    w_out = pl.BlockSpec((None, MOE_HIDDEN, D_MODEL), lambda i, be, bv, rd, nu: (be[i], 0, 0))
    return pl.pallas_call(
        _expert_kernel,
        grid_spec=pltpu.PrefetchScalarGridSpec(
            num_scalar_prefetch=4,
            grid=(n_blocks,),
            in_specs=[
                pl.BlockSpec((None, rb, 1), lambda i, be, bv, rd, nu: (i, 0, 0)),
                w_in, w_in, w_out,
                pl.BlockSpec(memory_space=pl.ANY),
            ],
            out_specs=pl.BlockSpec(memory_space=pl.ANY),
            scratch_shapes=[
                pltpu.VMEM((2, rb, D_MODEL), F32),
                pltpu.VMEM((2, rb, D_MODEL), F32),
                pltpu.SemaphoreType.DMA((2,)),
                pltpu.SemaphoreType.DMA((2,)),
            ],
        ),
        out_shape=jax.ShapeDtypeStruct((MOE_TOPK * n, D_MODEL), F32),
        compiler_params=pltpu.CompilerParams(
            dimension_semantics=("arbitrary",), vmem_limit_bytes=VMEM_LIMIT),
        name="experts",
    )(blk_exp, blk_valid, row_dst, n_used, row_gate, wg_bf, wu_bf, wd_bf, xn)


def _moe(xn, logits, b_group, b_fine, wg_bf, wu_bf, wd_bf):
    expert, gates = _route(logits, b_group, b_fine)
    y = _experts(xn, _plan_rows(expert, gates), wg_bf, wu_bf, wd_bf)
    return y.reshape(xn.shape[0], MOE_TOPK * D_MODEL)


def _proj_odd_kernel(h_ref, y_ref, g_ref, w_ref, rc_ref, rs_ref, hout_ref, q_ref, k_ref, v_ref):
    tm = h_ref.shape[0]
    h = h_ref[...] + (y_ref[:, :D_MODEL] + y_ref[:, D_MODEL:])
    hout_ref[...] = h
    xn = _rms(h, g_ref[...]).astype(BF16)
    lane = lax.broadcasted_iota(jnp.int32, (tm, LANES), 1)
    first_half = (lane % HEAD_DIM) < (HEAD_DIM // 2)
    rc, rs = rc_ref[...], rs_ref[...]
    scale = HEAD_DIM ** -0.5
    n_chunk = 4
    pairs = DIL_HEADS // 2
    for j in range(ODD_IN_WIDTH // (n_chunk * LANES)):
        acc = _dot(xn, w_ref[:, j * n_chunk * LANES:(j + 1) * n_chunk * LANES])
        for u in range(n_chunk):
            c = j * n_chunk + u
            a = acc[:, u * LANES:(u + 1) * LANES]
            if c < pairs:
                q_ref[c] = ((a * rc + _swap_half(a, first_half) * rs) * scale).astype(BF16)
            elif c < 2 * pairs:
                k_ref[c - pairs] = (a * rc + _swap_half(a, first_half) * rs).astype(BF16)
            else:
                v_ref[c - 2 * pairs] = a.astype(BF16)


def _proj_odd(h1, y2, gain, w_bf, batch, seq):
    n = h1.shape[0]
    tm = PROJ_ROWS
    pairs = DIL_HEADS // 2
    rc, rs = _rotary_tables(seq)
    per_seq = seq // tm
    tab = pl.BlockSpec((tm, LANES), lambda i: (i % per_seq, 0))
    hm = pl.BlockSpec((None, pairs, tm, LANES), lambda i: (i // per_seq, 0, i % per_seq, 0))
    hm_shape = jax.ShapeDtypeStruct((batch, pairs, seq, LANES), BF16)
    return pl.pallas_call(
        _proj_odd_kernel,
        grid=(n // tm,),
        in_specs=[
            pl.BlockSpec((tm, D_MODEL), lambda i: (i, 0)),
            pl.BlockSpec((tm, MOE_TOPK * D_MODEL), lambda i: (i, 0)),
            pl.BlockSpec((1, D_MODEL), lambda i: (0, 0)),
            pl.BlockSpec((D_MODEL, ODD_IN_WIDTH), lambda i: (0, 0)),
            tab, tab,
        ],
        out_specs=[pl.BlockSpec((tm, D_MODEL), lambda i: (i, 0)), hm, hm, hm],
        out_shape=[jax.ShapeDtypeStruct((n, D_MODEL), F32), hm_shape, hm_shape, hm_shape],
        compiler_params=pltpu.CompilerParams(
            dimension_semantics=("arbitrary",), vmem_limit_bytes=VMEM_LIMIT),
        name="proj_odd",
    )(h1, y2, gain.reshape(1, D_MODEL), w_bf, rc, rs)


def _dilated_kernel(q_ref, kp_ref, ko_ref, vp_ref, vo_ref, o_ref, l_ref):
    n = pl.program_id(3)
    blk = DIL_BLOCK
    a = lax.broadcasted_iota(jnp.int32, (blk, 2 * blk), 0)
    e = lax.broadcasted_iota(jnp.int32, (blk, 2 * blk), 1)
    dist = blk + a - e
    band = (dist >= 0) & (dist <= blk)
    first = band & ((n > 0) | (e >= blk))
    for u in range(q_ref.shape[0] // blk):
        rows = slice(u * blk, (u + 1) * blk)
        o_parts, l_parts = [], []
        for h in range(2):
            sl = slice(h * HEAD_DIM, (h + 1) * HEAD_DIM)
            q = q_ref[rows, sl]
            if u == 0:
                k = jnp.concatenate([kp_ref[:, sl], ko_ref[:blk, sl]], axis=0)
                v = jnp.concatenate([vp_ref[:, sl], vo_ref[:blk, sl]], axis=0)
                mask = first
            else:
                k = ko_ref[(u - 1) * blk:(u + 1) * blk, sl]
                v = vo_ref[(u - 1) * blk:(u + 1) * blk, sl]
                mask = band
            s = jnp.where(mask, _dot_nt(q, k), NEG_INF)
            m = jnp.max(s, axis=-1, keepdims=True)
            p = jnp.exp(s - m)
            den = jnp.sum(p, axis=-1, keepdims=True)
            o_parts.append(_dot((p / den).astype(BF16), v))
            l_parts.append(jnp.broadcast_to(m + jnp.log(den), (blk, HEAD_DIM)))
        o_ref[rows, :] = jnp.concatenate(o_parts, axis=-1)
        l_ref[rows, :] = jnp.concatenate(l_parts, axis=-1)


def _dilated_branch(q_hm, k_hm, v_hm, dil):
    b, pairs, s, _ = q_hm.shape
    length = s // dil
    tq = min(DIL_ROWS, length)
    sub = tq // DIL_BLOCK
    view = lambda t: t.reshape(b, pairs, length, dil * LANES)
    own = pl.BlockSpec((None, None, tq, LANES), lambda bi, hp, r, n: (bi, hp, n, r))
    prev = pl.BlockSpec((None, None, DIL_BLOCK, LANES),
                        lambda bi, hp, r, n: (bi, hp, jnp.maximum(n * sub - 1, 0), r))
    shape = jax.ShapeDtypeStruct((b, pairs, length, dil * LANES), F32)
    o, lse = pl.pallas_call(
        _dilated_kernel,
        grid=(b, pairs, dil, length // tq),
        in_specs=[own, prev, own, prev, own],
        out_specs=[own, own],
        out_shape=[shape, shape],
        compiler_params=pltpu.CompilerParams(
            dimension_semantics=("arbitrary",) * 4, vmem_limit_bytes=VMEM_LIMIT),
        name=f"dilated_{dil}",
    )(view(q_hm), view(k_hm), view(k_hm), view(v_hm), view(v_hm))
    return o.reshape(b, pairs, s, LANES), lse.reshape(b, pairs, s, LANES)


def _final_kernel(h_ref, y_ref, g_ref, o_ref):
    h = h_ref[...] + (y_ref[:, :D_MODEL] + y_ref[:, D_MODEL:])
    o_ref[...] = _rms(h, g_ref[...])


def _final(h, y2, gain):
    n = h.shape[0]
    tm = PROJ_ROWS
    return pl.pallas_call(
        _final_kernel,
        grid=(n // tm,),
        in_specs=[pl.BlockSpec((tm, D_MODEL), lambda i: (i, 0)),
                  pl.BlockSpec((tm, MOE_TOPK * D_MODEL), lambda i: (i, 0)),
                  pl.BlockSpec((1, D_MODEL), lambda i: (0, 0))],
        out_specs=pl.BlockSpec((tm, D_MODEL), lambda i: (i, 0)),
        out_shape=jax.ShapeDtypeStruct((n, D_MODEL), F32),
        compiler_params=pltpu.CompilerParams(
            dimension_semantics=("arbitrary",), vmem_limit_bytes=VMEM_LIMIT),
        name="final_norm",
    )(h, y2, gain.reshape(1, D_MODEL))


def _router_weights(w_group, w_fine):
    w = jnp.concatenate([w_group, w_fine], axis=-1).astype(F32)
    w = jnp.pad(w, ((0, 0), (0, ROUTER_WIDTH - w.shape[1])))
    w_hi = w.astype(BF16)
    return w_hi, (w - w_hi.astype(F32)).astype(BF16)


def kernel(x, mix_norm_even, w_in_even, w_out_even, mix_norm_odd, w_in_odd, w_out_odd, ffn_norm,
           w_router_group, b_router_group, w_router_expert, b_router_expert,
           w_expert_gate, w_expert_up, w_expert_down, final_norm):
    batch, seq, d = x.shape
    n = batch * seq
    x2 = x.reshape(n, d)

    qkv, rg, kmean = _proj_even(x2, mix_norm_even[0], w_in_even[0].astype(BF16), seq)
    qkv3 = qkv.reshape(batch, seq, EVEN_QKV_WIDTH)
    a_out = _moba(qkv3, kmean.reshape(batch, seq // MOBA_BLOCK, MOBA_WIDTH))
    r_out = _retention(qkv3, rg.reshape(batch, seq, RET_V_WIDTH))
    wr_hi, wr_lo = _router_weights(w_router_group[0], w_router_expert[0])
    h1, xn, logits = _outproj_even(x2, a_out.reshape(n, MOBA_WIDTH), r_out.reshape(n, RET_V_WIDTH),
                                   w_out_even[0].astype(BF16), ffn_norm[0], wr_hi, wr_lo)
    y = _moe(xn, logits, b_router_group[0], b_router_expert[0], w_expert_gate[0].astype(BF16),
             w_expert_up[0].astype(BF16), w_expert_down[0].astype(BF16))

    h2, q_hm, k_hm, v_hm = _proj_odd(h1, y, mix_norm_odd[0], w_in_odd[0].astype(BF16), batch, seq)
    branches = [_dilated_branch(q_hm, k_hm, v_hm, dil) for (_, dil) in DIL_PATTERNS]
    wr_hi, wr_lo = _router_weights(w_router_group[1], w_router_expert[1])
    h3, xn, logits = _outproj_odd(h2, branches, w_out_odd[0].astype(BF16), ffn_norm[1],
                                  wr_hi, wr_lo, seq)
    y = _moe(xn, logits, b_router_group[1], b_router_expert[1], w_expert_gate[1].astype(BF16),
             w_expert_up[1].astype(BF16), w_expert_down[1].astype(BF16))
    return _final(h3, y, final_norm).reshape(batch, seq, d)
```

```python
import functools

import jax
import jax.numpy as jnp
from jax import lax
from jax.experimental import pallas as pl
from jax.experimental.pallas import tpu as pltpu

F32 = jnp.float32
BF16 = jnp.bfloat16

D_MODEL = 1024
HEAD_DIM = 64
ROPE_THETA = 10000.0
RET_THETA = 10000.0
NORM_EPS = 1e-6

MOBA_HEADS = 8
MOBA_BLOCK = 256
MOBA_TOPK = 3
MOBA_CHUNK = 16
MOBA_WIDTH = MOBA_HEADS * HEAD_DIM
RET_HEADS = 4
RET_QK_DIM = 64
RET_V_DIM = 128
RET_CHUNK = 128
RET_QK_WIDTH = RET_HEADS * RET_QK_DIM
RET_V_WIDTH = RET_HEADS * RET_V_DIM
EVEN_IN_WIDTH = 3 * MOBA_WIDTH + 2 * RET_QK_WIDTH + 2 * RET_V_WIDTH
EVEN_QKV_WIDTH = EVEN_IN_WIDTH - RET_V_WIDTH

DIL_HEADS = D_MODEL // HEAD_DIM
DIL_PATTERNS = ((128, 1), (512, 4), (2048, 16))
DIL_BLOCK = 128
ODD_IN_WIDTH = 3 * D_MODEL

MOE_GROUPS = 4
MOE_EXPERTS_PER_GROUP = 8
MOE_EXPERTS = MOE_GROUPS * MOE_EXPERTS_PER_GROUP
MOE_TOPK = 2
MOE_HIDDEN = D_MODEL // 2

LANES = 128
ROUTER_WIDTH = LANES
PROJ_ROWS = 512
EXPERT_ROWS = 256
DIL_ROWS = 512
RET_ROWS = 512
VMEM_LIMIT = 56 * 1024 * 1024
NEG_INF = float("-inf")


def _dot(a, b):
    return jnp.dot(a, b, preferred_element_type=F32)


def _dot_nt(a, b):
    return lax.dot_general(a, b, (((1,), (1,)), ((), ())), preferred_element_type=F32)


def _dot_tn(a, b):
    return lax.dot_general(a, b, (((0,), (0,)), ((), ())), preferred_element_type=F32)


def _rms(x, gain):
    ms = jnp.mean(x * x, axis=-1, keepdims=True)
    return x * lax.rsqrt(ms + NORM_EPS) * gain


def _swap_half(a, first_half):
    return jnp.where(first_half, pltpu.roll(a, LANES - 32, 1), pltpu.roll(a, 32, 1))


def _swap_pairs(a, even):
    return jnp.where(even, pltpu.roll(a, LANES - 1, 1), pltpu.roll(a, 1, 1))


def _rotary_tables(s):
    half = HEAD_DIM // 2
    inv_freq = 1.0 / (ROPE_THETA ** (jnp.arange(half, dtype=F32) * (2.0 / HEAD_DIM)))
    ang = jnp.arange(s, dtype=F32)[:, None] * inv_freq[None, :]
    cos, sin = jnp.cos(ang), jnp.sin(ang)
    return (jnp.concatenate([cos, cos, cos, cos], axis=-1),
            jnp.concatenate([-sin, sin, -sin, sin], axis=-1))


def _retnet_tables(s):
    half = RET_QK_DIM // 2
    freq = 1.0 / (RET_THETA ** jnp.linspace(0.0, 1.0, half, dtype=F32))
    ang = jnp.arange(s, dtype=F32)[:, None] * freq[None, :]
    cos, sin = jnp.cos(ang), jnp.sin(ang)
    cos2 = jnp.repeat(cos, 2, axis=-1)
    sin2 = jnp.stack([-sin, sin], axis=-1).reshape(s, RET_QK_DIM)
    return jnp.tile(cos2, (1, 2)), jnp.tile(sin2, (1, 2))


def _retention_tables():
    c = RET_CHUNK
    log_gamma = jnp.log(1.0 - 2.0 ** (-5.0 - jnp.arange(RET_HEADS, dtype=F32)))
    pos = jnp.arange(c, dtype=F32)
    rel = pos[:, None] - pos[None, :]
    decay = jnp.where(rel >= 0, jnp.exp(log_gamma[:, None, None] * jnp.maximum(rel, 0.0)), 0.0)
    tail = jnp.exp(log_gamma[:, None] * (c - 1.0 - pos))[:, :, None]
    head = jnp.exp(log_gamma[:, None] * (pos + 1.0))[:, :, None]
    gamma_c = jnp.exp(log_gamma * c)[:, None, None]
    return decay, tail, head, gamma_c


def _stack_blocks(a, lower_lanes):
    top, bot = a[:MOBA_BLOCK], a[MOBA_BLOCK:]
    head_a = jnp.where(lower_lanes, top, pltpu.roll(bot, HEAD_DIM, 1))
    head_b = jnp.where(lower_lanes, pltpu.roll(top, HEAD_DIM, 1), bot)
    return head_a.astype(BF16), head_b.astype(BF16)


def _proj_even_kernel(x_ref, g_ref, w_ref, rc_ref, rs_ref, tc_ref, ts_ref,
                      qkv_ref, rg_ref, km_ref, kst_ref, vst_ref):
    tm = x_ref.shape[0]
    xn = _rms(x_ref[...], g_ref[...]).astype(BF16)
    lane = lax.broadcasted_iota(jnp.int32, (tm, LANES), 1)
    first_half = (lane % HEAD_DIM) < (HEAD_DIM // 2)
    even = (lane % 2) == 0
    lower_lanes = lax.broadcasted_iota(jnp.int32, (MOBA_BLOCK, LANES), 1) < HEAD_DIM
    rc, rs, tc, ts = rc_ref[...], rs_ref[...], tc_ref[...], ts_ref[...]
    scale = HEAD_DIM ** -0.5
    n_chunk = 4
    for j in range(EVEN_IN_WIDTH // (n_chunk * LANES)):
        acc = _dot(xn, w_ref[:, j * n_chunk * LANES:(j + 1) * n_chunk * LANES])
        for u in range(n_chunk):
            c = j * n_chunk + u
            a = acc[:, u * LANES:(u + 1) * LANES]
            cols = slice(c * LANES, (c + 1) * LANES)
            if c < 4:
                a = (a * rc + _swap_half(a, first_half) * rs) * scale
            elif c < 8:
                a = a * rc + _swap_half(a, first_half) * rs
                km_ref[0, :, (c - 4) * LANES:(c - 3) * LANES] = jnp.mean(
                    a.reshape(tm // MOBA_BLOCK, MOBA_BLOCK, LANES), axis=1)
                kst_ref[2 * (c - 4)], kst_ref[2 * (c - 4) + 1] = _stack_blocks(a, lower_lanes)
            elif c < 12:
                vst_ref[2 * (c - 8)], vst_ref[2 * (c - 8) + 1] = _stack_blocks(a, lower_lanes)
            elif c < 14:
                a = a * tc + _swap_pairs(a, even) * ts
            elif c < 16:
                a = (a * tc + _swap_pairs(a, even) * ts) * (RET_QK_DIM ** -0.5)
            elif c < 20:
                pass
            else:
                rg_ref[:, (c - 20) * LANES:(c - 19) * LANES] = a
                continue
            qkv_ref[:, cols] = a.astype(BF16)


def _proj_even(x2, gain, w_bf, batch, seq):
    n = x2.shape[0]
    tm = PROJ_ROWS
    assert tm == 2 * MOBA_BLOCK and seq % (MOBA_CHUNK * MOBA_BLOCK) == 0
    rc, rs = _rotary_tables(seq)
    tc, ts = _retnet_tables(seq)
    pos_blocks = seq // tm
    nblk = seq // MOBA_BLOCK
    tab = pl.BlockSpec((tm, LANES), lambda i: (i % pos_blocks, 0))
    tiles_per_chunk = MOBA_CHUNK // 2
    kst_spec = pl.BlockSpec(
        (None, MOBA_HEADS, None, MOBA_BLOCK, LANES),
        lambda i: (i // pos_blocks, 0, (i % pos_blocks) // tiles_per_chunk, 0,
                   (i % pos_blocks) % tiles_per_chunk))
    vst_spec = pl.BlockSpec(
        (None, MOBA_HEADS, None, MOBA_BLOCK, LANES),
        lambda i: (i // pos_blocks, 0, (i % pos_blocks) // 2, 0, (i % pos_blocks) % 2))
    return pl.pallas_call(
        _proj_even_kernel,
        grid=(n // tm,),
        in_specs=[
            pl.BlockSpec((tm, D_MODEL), lambda i: (i, 0)),
            pl.BlockSpec((1, D_MODEL), lambda i: (0, 0)),
            pl.BlockSpec((D_MODEL, EVEN_IN_WIDTH), lambda i: (0, 0)),
            tab, tab, tab, tab,
        ],
        out_specs=[
            pl.BlockSpec((tm, EVEN_QKV_WIDTH), lambda i: (i, 0)),
            pl.BlockSpec((tm, RET_V_WIDTH), lambda i: (i, 0)),
            pl.BlockSpec((1, tm // MOBA_BLOCK, MOBA_WIDTH), lambda i: (i, 0, 0)),
            kst_spec, vst_spec,
        ],
        out_shape=[
            jax.ShapeDtypeStruct((n, EVEN_QKV_WIDTH), BF16),
            jax.ShapeDtypeStruct((n, RET_V_WIDTH), F32),
            jax.ShapeDtypeStruct((n // tm, tm // MOBA_BLOCK, MOBA_WIDTH), F32),
            jax.ShapeDtypeStruct((batch, MOBA_HEADS, nblk // MOBA_CHUNK, MOBA_BLOCK,
                                  MOBA_CHUNK * HEAD_DIM), BF16),
            jax.ShapeDtypeStruct((batch, MOBA_HEADS, nblk // 4, MOBA_BLOCK, 4 * HEAD_DIM), BF16),
        ],
        compiler_params=pltpu.CompilerParams(
            dimension_semantics=("arbitrary",), vmem_limit_bytes=VMEM_LIMIT),
        name="proj_even",
    )(x2, gain.reshape(1, D_MODEL), w_bf, rc, rs, tc, ts)


def _moba_kernel(q_ref, ko_ref, vo_ref, km_ref, kst_ref, vst_ref, o_ref):
    i = pl.program_id(2)
    nblk = km_ref.shape[0]
    bq = MOBA_BLOCK
    kw = MOBA_CHUNK * HEAD_DIM
    n_chunks = (i + MOBA_CHUNK - 1) // MOBA_CHUNK
    blk = lax.broadcasted_iota(jnp.int32, (bq, nblk), 1)
    row = lax.broadcasted_iota(jnp.int32, (bq, bq), 0)
    col = lax.broadcasted_iota(jnp.int32, (bq, bq), 1)
    lane_kw = lax.broadcasted_iota(jnp.int32, (1, kw), 1)
    lane_blk = (lane_kw // HEAD_DIM).astype(F32)
    lane_grp = (lane_kw // MOBA_BLOCK).astype(F32)
    lane_sub = lax.broadcasted_iota(jnp.int32, (1, bq), 1) // HEAD_DIM
    outs = []
    for h in range(2):
        sl = slice(h * HEAD_DIM, (h + 1) * HEAD_DIM)
        q = q_ref[:, sl]
        km = km_ref[:, sl]
        km_hi = km.astype(BF16)
        km_lo = (km - km_hi.astype(F32)).astype(BF16)
        gate = _dot_nt(q, km_hi) + _dot_nt(q, km_lo)
        g = jnp.where(blk < i, gate, NEG_INF)
        sels = []
        for _ in range(MOBA_TOPK):
            m = jnp.max(g, axis=-1, keepdims=True)
            cand = (g == m) & (m > NEG_INF)
            idx = jnp.min(jnp.where(cand, blk, nblk), axis=-1, keepdims=True)
            g = jnp.where(blk == idx, NEG_INF, g)
            sels.append(jnp.where(idx < nblk, idx, -1))
        sel = jnp.concatenate(sels, axis=0)

        s_own = jnp.where(col <= row, _dot_nt(q, ko_ref[:, sl]), NEG_INF)

        q2 = jnp.concatenate([q, q], axis=1)
        qt = jnp.tile(jnp.concatenate([q2] * MOBA_TOPK, axis=0), (1, kw // LANES))
        sel_blk = sel.astype(F32).astype(BF16)

        def scores(c, acc, h=h, qt=qt, sel_blk=sel_blk):
            ids = (lane_blk + (c * MOBA_CHUNK).astype(F32)).astype(BF16)
            lhs = jnp.where(sel_blk == ids, qt, jnp.zeros_like(qt))
            return acc + _dot_nt(lhs, kst_ref[h, c])

        s_sel = lax.fori_loop(0, n_chunks, scores, jnp.zeros((MOBA_TOPK * bq, bq), F32))
        s_sel = jnp.where(sel >= 0, s_sel, NEG_INF)
        m = jnp.max(s_own, axis=-1, keepdims=True)
        for r in range(MOBA_TOPK):
            m = jnp.maximum(m, jnp.max(s_sel[r * bq:(r + 1) * bq], axis=-1, keepdims=True))
        p_own = jnp.exp(s_own - m)
        p_sel = jnp.exp(s_sel - jnp.concatenate([m] * MOBA_TOPK, axis=0))
        den = jnp.sum(p_own, axis=-1, keepdims=True)
        for r in range(MOBA_TOPK):
            den = den + jnp.sum(p_sel[r * bq:(r + 1) * bq], axis=-1, keepdims=True)

        pt = jnp.tile(p_sel.astype(BF16), (1, kw // bq))
        sel_grp = lax.shift_right_arithmetic(sel, 2).astype(F32).astype(BF16)

        def values(c, acc, h=h, pt=pt, sel_grp=sel_grp):
            ids = (lane_grp + (c * (MOBA_CHUNK // 4)).astype(F32)).astype(BF16)
            lhs = jnp.where(sel_grp == ids, pt, jnp.zeros_like(pt))
            return acc + _dot(lhs, vst_ref[h, c])

        o4 = lax.fori_loop(0, n_chunks, values, jnp.zeros((MOBA_TOPK * bq, bq), F32))
        o4 = jnp.where(lane_sub == (sel & 3), o4, 0.0)
        x = o4[:bq]
        for r in range(1, MOBA_TOPK):
            x = x + o4[r * bq:(r + 1) * bq]
        y = x[:, :LANES] + x[:, LANES:]
        y = y + pltpu.roll(y, HEAD_DIM, 1)
        outs.append((_dot(p_own.astype(BF16), vo_ref[:, sl]) + y[:, :HEAD_DIM]) / den)
    o_ref[...] = jnp.concatenate(outs, axis=-1).astype(BF16)


def _moba(qkv3, kmean3, kst, vst):
    b, s, _ = qkv3.shape
    nblk = s // MOBA_BLOCK
    pairs = MOBA_HEADS // 2
    n_chunks = nblk // MOBA_CHUNK
    kw = MOBA_CHUNK * HEAD_DIM
    own = lambda t: pl.BlockSpec((None, MOBA_BLOCK, LANES), lambda bi, hp, i: (bi, i, t * pairs + hp))
    return pl.pallas_call(
        _moba_kernel,
        grid=(b, pairs, nblk),
        in_specs=[
            own(0), own(1), own(2),
            pl.BlockSpec((None, nblk, LANES), lambda bi, hp, i: (bi, 0, hp)),
            pl.BlockSpec((None, 2, n_chunks, MOBA_BLOCK, kw), lambda bi, hp, i: (bi, hp, 0, 0, 0)),
            pl.BlockSpec((None, 2, n_chunks, kw, MOBA_BLOCK), lambda bi, hp, i: (bi, hp, 0, 0, 0)),
        ],
        out_specs=own(0),
        out_shape=jax.ShapeDtypeStruct((b, s, MOBA_WIDTH), BF16),
        compiler_params=pltpu.CompilerParams(
            dimension_semantics=("arbitrary", "arbitrary", "arbitrary"),
            vmem_limit_bytes=VMEM_LIMIT),
        name="moba",
    )(qkv3, qkv3, qkv3, kmean3, kst, vst.reshape(b, MOBA_HEADS, n_chunks, kw, MOBA_BLOCK))


def _retention_kernel(q_ref, k_ref, v_ref, g_ref, dec_ref, tail_ref, head_ref, gc_ref,
                      o_ref, state_ref):
    @pl.when(pl.program_id(2) == 0)
    def _():
        state_ref[...] = jnp.zeros_like(state_ref)

    c = RET_CHUNK
    for h in range(2):
        qk = slice(h * RET_QK_DIM, (h + 1) * RET_QK_DIM)
        vv = slice(h * RET_V_DIM, (h + 1) * RET_V_DIM)
        dec, tail, head, gc = dec_ref[h], tail_ref[h], head_ref[h], gc_ref[h]
        for u in range(q_ref.shape[0] // c):
            rows = slice(u * c, (u + 1) * c)
            q, k, v = q_ref[rows, qk], k_ref[rows, qk], v_ref[rows, vv]
            scores = _dot_nt(q, k) * dec
            state = state_ref[h]
            y = (_dot(scores.astype(BF16), v)
                 + _dot((q.astype(F32) * head).astype(BF16), state.astype(BF16)))
            state_ref[h] = gc * state + _dot_tn((k.astype(F32) * tail).astype(BF16), v)
            mu = jnp.mean(y, axis=-1, keepdims=True)
            yc = y - mu
            var = jnp.mean(yc * yc, axis=-1, keepdims=True)
            gate = g_ref[rows, vv]
            swish = gate / (1.0 + jnp.exp(-gate))
            o_ref[rows, vv] = (yc * lax.rsqrt(var + NORM_EPS) * swish).astype(BF16)


def _retention(qkv3, rg3):
    b, s, _ = qkv3.shape
    t = RET_ROWS
    pairs = RET_HEADS // 2
    dec, tail, head, gc = _retention_tables()
    q0 = 3 * MOBA_WIDTH // LANES
    k0 = q0 + RET_QK_WIDTH // LANES
    v0 = (3 * MOBA_WIDTH + 2 * RET_QK_WIDTH) // (2 * RET_V_DIM)
    return pl.pallas_call(
        _retention_kernel,
        grid=(b, pairs, s // t),
        in_specs=[
            pl.BlockSpec((None, t, LANES), lambda bi, hp, n: (bi, n, q0 + hp)),
            pl.BlockSpec((None, t, LANES), lambda bi, hp, n: (bi, n, k0 + hp)),
            pl.BlockSpec((None, t, 2 * RET_V_DIM), lambda bi, hp, n: (bi, n, v0 + hp)),
            pl.BlockSpec((None, t, 2 * RET_V_DIM), lambda bi, hp, n: (bi, n, hp)),
            pl.BlockSpec((2, RET_CHUNK, RET_CHUNK), lambda bi, hp, n: (hp, 0, 0)),
            pl.BlockSpec((2, RET_CHUNK, 1), lambda bi, hp, n: (hp, 0, 0)),
            pl.BlockSpec((2, RET_CHUNK, 1), lambda bi, hp, n: (hp, 0, 0)),
            pl.BlockSpec((2, 1, 1), lambda bi, hp, n: (hp, 0, 0)),
        ],
        out_specs=pl.BlockSpec((None, t, 2 * RET_V_DIM), lambda bi, hp, n: (bi, n, hp)),
        out_shape=jax.ShapeDtypeStruct((b, s, RET_V_WIDTH), BF16),
        scratch_shapes=[pltpu.VMEM((2, RET_QK_DIM, RET_V_DIM), F32)],
        compiler_params=pltpu.CompilerParams(
            dimension_semantics=("arbitrary", "arbitrary", "arbitrary"),
            vmem_limit_bytes=VMEM_LIMIT),
        name="retention",
    )(qkv3, qkv3, qkv3, rg3, dec, tail, head, gc)


def _route(xn, wr_hi_ref, wr_lo_ref, br_ref):
    x_hi = xn.astype(BF16)
    x_lo = (xn - x_hi.astype(F32)).astype(BF16)
    w_hi = wr_hi_ref[...]
    z = _dot(x_hi, w_hi) + _dot(x_lo, w_hi) + _dot(x_hi, wr_lo_ref[...]) + br_ref[...]
    lane = lax.broadcasted_iota(jnp.int32, z.shape, 1)

    def top1(v):
        m = jnp.max(v, axis=-1, keepdims=True)
        return m, jnp.min(jnp.where(v == m, lane, ROUTER_WIDTH), axis=-1, keepdims=True)

    g_logit = jnp.where(lane < MOE_GROUPS, z, NEG_INF)
    g_max, grp = top1(g_logit)
    g_val = 1.0 / jnp.sum(jnp.exp(g_logit - g_max), axis=-1, keepdims=True)
    lo = MOE_GROUPS + MOE_EXPERTS_PER_GROUP * grp
    f = jnp.where((lane >= lo) & (lane < lo + MOE_EXPERTS_PER_GROUP), z, NEG_INF)
    v1, i1 = top1(f)
    v2, i2 = top1(jnp.where(lane == i1, NEG_INF, f))
    t = jnp.exp(v2 - v1)
    g1 = g_val / (1.0 + t)
    g2 = g_val * t / (1.0 + t)
    e1 = (i1 - MOE_GROUPS).astype(F32)
    e2 = (i2 - MOE_GROUPS).astype(F32)
    return jnp.where(lane == 0, e1, jnp.where(lane == 1, e2, jnp.where(
        lane == 2, g1, jnp.where(lane == 3, g2, 0.0))))


def _outproj_even_kernel(x_ref, a_ref, r_ref, wo_ref, gn_ref, wr_hi_ref, wr_lo_ref, br_ref,
                         h_ref, xn_ref, rt_ref):
    h = x_ref[...] + (_dot(a_ref[...], wo_ref[:MOBA_WIDTH, :])
                      + _dot(r_ref[...], wo_ref[MOBA_WIDTH:, :]))
    h_ref[...] = h
    xn = _rms(h, gn_ref[...])
    xn_ref[...] = xn
    rt_ref[...] = _route(xn, wr_hi_ref, wr_lo_ref, br_ref)


def _outproj_even(x2, a2, r2, wo_bf, gain, router):
    n = x2.shape[0]
    tm = PROJ_ROWS
    row = lambda w: pl.BlockSpec((tm, w), lambda i: (i, 0))
    full = lambda r, w: pl.BlockSpec((r, w), lambda i: (0, 0))
    return pl.pallas_call(
        _outproj_even_kernel,
        grid=(n // tm,),
        in_specs=[row(D_MODEL), row(MOBA_WIDTH), row(RET_V_WIDTH), full(D_MODEL, D_MODEL),
                  full(1, D_MODEL), full(D_MODEL, ROUTER_WIDTH), full(D_MODEL, ROUTER_WIDTH),
                  full(1, ROUTER_WIDTH)],
        out_specs=[row(D_MODEL), row(D_MODEL), row(ROUTER_WIDTH)],
        out_shape=[jax.ShapeDtypeStruct((n, D_MODEL), F32),
                   jax.ShapeDtypeStruct((n, D_MODEL), F32),
                   jax.ShapeDtypeStruct((n, ROUTER_WIDTH), F32)],
        compiler_params=pltpu.CompilerParams(
            dimension_semantics=("arbitrary",), vmem_limit_bytes=VMEM_LIMIT),
        name="outproj_even",
    )(x2, a2, r2, wo_bf, gain.reshape(1, D_MODEL), *router)


def _outproj_odd_kernel(h_ref, o1_ref, o2_ref, o3_ref, l1_ref, l2_ref, l3_ref, wo_ref,
                        gn_ref, wr_hi_ref, wr_lo_ref, br_ref, hout_ref, xn_ref, rt_ref):
    pieces = []
    for hp in range(DIL_HEADS // 2):
        la, lb, lc = l1_ref[hp], l2_ref[hp], l3_ref[hp]
        m = jnp.maximum(jnp.maximum(la, lb), lc)
        ea, eb, ec = jnp.exp(la - m), jnp.exp(lb - m), jnp.exp(lc - m)
        mix = (ea * o1_ref[hp] + eb * o2_ref[hp] + ec * o3_ref[hp]) / (ea + eb + ec)
        pieces.append(mix.astype(BF16))
    h = h_ref[...] + _dot(jnp.concatenate(pieces, axis=-1), wo_ref[...])
    hout_ref[...] = h
    xn = _rms(h, gn_ref[...])
    xn_ref[...] = xn
    rt_ref[...] = _route(xn, wr_hi_ref, wr_lo_ref, br_ref)


def _outproj_odd(h2, branches, wo_bf, gain, router, seq):
    n = h2.shape[0]
    tm = PROJ_ROWS
    pairs = DIL_HEADS // 2
    per_seq = seq // tm
    row = lambda w: pl.BlockSpec((tm, w), lambda i: (i, 0))
    full = lambda r, w: pl.BlockSpec((r, w), lambda i: (0, 0))
    hm = pl.BlockSpec((None, pairs, tm, LANES), lambda i: (i // per_seq, 0, i % per_seq, 0))
    outs = [o for (o, _) in branches]
    lses = [l for (_, l) in branches]
    return pl.pallas_call(
        _outproj_odd_kernel,
        grid=(n // tm,),
        in_specs=[row(D_MODEL), hm, hm, hm, hm, hm, hm, full(D_MODEL, D_MODEL),
                  full(1, D_MODEL), full(D_MODEL, ROUTER_WIDTH), full(D_MODEL, ROUTER_WIDTH),
                  full(1, ROUTER_WIDTH)],
        out_specs=[row(D_MODEL), row(D_MODEL), row(ROUTER_WIDTH)],
        out_shape=[jax.ShapeDtypeStruct((n, D_MODEL), F32),
                   jax.ShapeDtypeStruct((n, D_MODEL), F32),
                   jax.ShapeDtypeStruct((n, ROUTER_WIDTH), F32)],
        compiler_params=pltpu.CompilerParams(
            dimension_semantics=("arbitrary",), vmem_limit_bytes=VMEM_LIMIT),
        name="outproj_odd",
    )(h2, *outs, *lses, wo_bf, gain.reshape(1, D_MODEL), *router)


def _plan_rows(expert):
    rb = EXPERT_ROWS
    n_assign = expert.size
    e_flat = expert.reshape(-1)
    order = jnp.argsort(e_flat).astype(jnp.int32)
    e_s = e_flat[order]
    counts = jnp.bincount(e_flat, length=MOE_EXPERTS).astype(jnp.int32)
    starts = jnp.cumsum(counts) - counts
    padded = ((counts + rb - 1) // rb) * rb
    p_ends = jnp.cumsum(padded)
    p_starts = p_ends - padded
    dest = p_starts[e_s] + (jnp.arange(n_assign, dtype=jnp.int32) - starts[e_s])
    n_blocks = n_assign // rb + MOE_EXPERTS
    n_rows = n_blocks * rb
    row_dst = jnp.zeros((n_rows,), jnp.int32).at[dest].set(order)
    blk_start = jnp.arange(n_blocks, dtype=jnp.int32) * rb
    blk_exp = jnp.minimum(jnp.searchsorted(p_ends, blk_start, side='right'),
                          MOE_EXPERTS - 1).astype(jnp.int32)
    blk_valid = jnp.clip(counts[blk_exp] - (blk_start - p_starts[blk_exp]), 0, rb).astype(jnp.int32)
    n_used = (p_ends[-1] // rb).astype(jnp.int32).reshape(1)
    return blk_exp, blk_valid, row_dst, n_used


def _expert_kernel(blk_exp, blk_valid, row_dst, n_used,
                   wg_ref, wu_ref, wd_ref, x_hbm, out_hbm,
                   xbuf, ybuf, gsem, ssem):
    del blk_exp
    rb = EXPERT_ROWS
    i = pl.program_id(0)
    used = n_used[0]
    slot = i % 2

    def gather_copy(tok, r, s):
        return pltpu.make_async_copy(x_hbm.at[pl.ds(tok, 1)], xbuf.at[s, pl.ds(r, 1)], gsem.at[s])

    def scatter_copy(dst, r, s):
        return pltpu.make_async_copy(ybuf.at[s, pl.ds(r, 1)], out_hbm.at[pl.ds(dst, 1)], ssem.at[s])

    def start_gather(b, s):
        def body(r, carry):
            gather_copy(row_dst[b * rb + r] >> 1, r, s).start()
            return carry
        lax.fori_loop(0, rb, body, 0)

    def wait_scatter(count, s):
        def body(r, carry):
            scatter_copy(0, 0, s).wait()
            return carry
        lax.fori_loop(0, count, body, 0)

    @pl.when(i == 0)
    def _():
        start_gather(0, 0)

    @pl.when(i < used)
    def _():
        @pl.when(i + 1 < used)
        def _():
            start_gather(i + 1, 1 - slot)

        def wait_body(r, carry):
            gather_copy(0, 0, slot).wait()
            return carry
        lax.fori_loop(0, rb, wait_body, 0)

        @pl.when(i >= 2)
        def _():
            wait_scatter(blk_valid[i - 2], slot)

        xb = xbuf[slot].astype(BF16)
        g = _dot(xb, wg_ref[...])
        u = _dot(xb, wu_ref[...])
        hid = (g / (1.0 + jnp.exp(-g)) * u).astype(BF16)
        ybuf[slot] = _dot(hid, wd_ref[...])

        def scatter_body(r, carry):
            scatter_copy(row_dst[i * rb + r], r, slot).start()
            return carry
        lax.fori_loop(0, blk_valid[i], scatter_body, 0)

        @pl.when(i == used - 1)
        def _():
            wait_scatter(blk_valid[i], slot)

            @pl.when(i >= 1)
            def _():
                wait_scatter(blk_valid[i - 1], 1 - slot)


def _experts(xn, plan, wg_bf, wu_bf, wd_bf):
    blk_exp, blk_valid, row_dst, n_used = plan
    n = xn.shape[0]
    rb = EXPERT_ROWS
    n_blocks = blk_exp.shape[0]
    w_in = pl.BlockSpec((None, D_MODEL, MOE_HIDDEN), lambda i, be, bv, rd, nu: (be[i], 0, 0))
    w_out = pl.BlockSpec((None, MOE_HIDDEN, D_MODEL), lambda i, be, bv, rd, nu: (be[i], 0, 0))
    return pl.pallas_call(
        _expert_kernel,
        grid_spec=pltpu.PrefetchScalarGridSpec(
            num_scalar_prefetch=4,
            grid=(n_blocks,),
            in_specs=[
                w_in, w_in, w_out,
                pl.BlockSpec(memory_space=pl.ANY),
            ],
            out_specs=pl.BlockSpec(memory_space=pl.ANY),
            scratch_shapes=[
                pltpu.VMEM((2, rb, D_MODEL), F32),
                pltpu.VMEM((2, rb, D_MODEL), F32),
                pltpu.SemaphoreType.DMA((2,)),
                pltpu.SemaphoreType.DMA((2,)),
            ],
        ),
        out_shape=jax.ShapeDtypeStruct((MOE_TOPK * n, D_MODEL), F32),
        compiler_params=pltpu.CompilerParams(
            dimension_semantics=("arbitrary",), vmem_limit_bytes=VMEM_LIMIT),
        name="experts",
    )(blk_exp, blk_valid, row_dst, n_used, wg_bf, wu_bf, wd_bf, xn)


def _moe(xn, route, wg_bf, wu_bf, wd_bf):
    expert = route[:, :MOE_TOPK].astype(jnp.int32)
    y = _experts(xn, _plan_rows(expert), wg_bf, wu_bf, wd_bf)
    return y.reshape(xn.shape[0], MOE_TOPK * D_MODEL)


def _moe_combine(h_ref, y_ref, rt_ref):
    rt = rt_ref[...]
    return h_ref[...] + (y_ref[:, :D_MODEL] * rt[:, 2:3] + y_ref[:, D_MODEL:] * rt[:, 3:4])


def _proj_odd_kernel(h_ref, y_ref, rt_ref, g_ref, w_ref, rc_ref, rs_ref,
                     hout_ref, q_ref, k_ref, v_ref):
    tm = h_ref.shape[0]
    h = _moe_combine(h_ref, y_ref, rt_ref)
    hout_ref[...] = h
    xn = _rms(h, g_ref[...]).astype(BF16)
    lane = lax.broadcasted_iota(jnp.int32, (tm, LANES), 1)
    first_half = (lane % HEAD_DIM) < (HEAD_DIM // 2)
    rc, rs = rc_ref[...], rs_ref[...]
    scale = HEAD_DIM ** -0.5
    n_chunk = 4
    pairs = DIL_HEADS // 2
    for j in range(ODD_IN_WIDTH // (n_chunk * LANES)):
        acc = _dot(xn, w_ref[:, j * n_chunk * LANES:(j + 1) * n_chunk * LANES])
        for u in range(n_chunk):
            c = j * n_chunk + u
            a = acc[:, u * LANES:(u + 1) * LANES]
            if c < pairs:
                q_ref[c] = ((a * rc + _swap_half(a, first_half) * rs) * scale).astype(BF16)
            elif c < 2 * pairs:
                k_ref[c - pairs] = (a * rc + _swap_half(a, first_half) * rs).astype(BF16)
            else:
                v_ref[c - 2 * pairs] = a.astype(BF16)


def _proj_odd(h1, y2, route, gain, w_bf, batch, seq):
    n = h1.shape[0]
    tm = PROJ_ROWS
    pairs = DIL_HEADS // 2
    rc, rs = _rotary_tables(seq)
    per_seq = seq // tm
    tab = pl.BlockSpec((tm, LANES), lambda i: (i % per_seq, 0))
    hm = pl.BlockSpec((None, pairs, tm, LANES), lambda i: (i // per_seq, 0, i % per_seq, 0))
    hm_shape = jax.ShapeDtypeStruct((batch, pairs, seq, LANES), BF16)
    return pl.pallas_call(
        _proj_odd_kernel,
        grid=(n // tm,),
        in_specs=[
            pl.BlockSpec((tm, D_MODEL), lambda i: (i, 0)),
            pl.BlockSpec((tm, MOE_TOPK * D_MODEL), lambda i: (i, 0)),
            pl.BlockSpec((tm, ROUTER_WIDTH), lambda i: (i, 0)),
            pl.BlockSpec((1, D_MODEL), lambda i: (0, 0)),
            pl.BlockSpec((D_MODEL, ODD_IN_WIDTH), lambda i: (0, 0)),
            tab, tab,
        ],
        out_specs=[pl.BlockSpec((tm, D_MODEL), lambda i: (i, 0)), hm, hm, hm],
        out_shape=[jax.ShapeDtypeStruct((n, D_MODEL), F32), hm_shape, hm_shape, hm_shape],
        compiler_params=pltpu.CompilerParams(
            dimension_semantics=("arbitrary",), vmem_limit_bytes=VMEM_LIMIT),
        name="proj_odd",
    )(h1, y2, route, gain.reshape(1, D_MODEL), w_bf, rc, rs)


def _dilated_kernel(q_ref, kp_ref, ko_ref, vp_ref, vo_ref, o_ref, l_ref):
    n = pl.program_id(3)
    blk = DIL_BLOCK
    a = lax.broadcasted_iota(jnp.int32, (blk, 2 * blk), 0)
    e = lax.broadcasted_iota(jnp.int32, (blk, 2 * blk), 1)
    dist = blk + a - e
    band = (dist >= 0) & (dist <= blk)
    first = band & ((n > 0) | (e >= blk))
    for u in range(q_ref.shape[0] // blk):
        rows = slice(u * blk, (u + 1) * blk)
        o_parts, l_parts = [], []
        for h in range(2):
            sl = slice(h * HEAD_DIM, (h + 1) * HEAD_DIM)
            q = q_ref[rows, sl]
            if u == 0:
                k = jnp.concatenate([kp_ref[:, sl], ko_ref[:blk, sl]], axis=0)
                v = jnp.concatenate([vp_ref[:, sl], vo_ref[:blk, sl]], axis=0)
                mask = first
            else:
                k = ko_ref[(u - 1) * blk:(u + 1) * blk, sl]
                v = vo_ref[(u - 1) * blk:(u + 1) * blk, sl]
                mask = band
            s = jnp.where(mask, _dot_nt(q, k), NEG_INF)
            m = jnp.max(s, axis=-1, keepdims=True)
            p = jnp.exp(s - m)
            den = jnp.sum(p, axis=-1, keepdims=True)
            o_parts.append(_dot((p / den).astype(BF16), v))
            l_parts.append(jnp.broadcast_to(m + jnp.log(den), (blk, HEAD_DIM)))
        o_ref[rows, :] = jnp.concatenate(o_parts, axis=-1)
        l_ref[rows, :] = jnp.concatenate(l_parts, axis=-1)


def _dilated_branch(q_hm, k_hm, v_hm, dil):
    b, pairs, s, _ = q_hm.shape
    length = s // dil
    tq = min(DIL_ROWS, length)
    sub = tq // DIL_BLOCK
    view = lambda t: t.reshape(b, pairs, length, dil * LANES)
    own = pl.BlockSpec((None, None, tq, LANES), lambda bi, hp, r, n: (bi, hp, n, r))
    prev = pl.BlockSpec((None, None, DIL_BLOCK, LANES),
                        lambda bi, hp, r, n: (bi, hp, jnp.maximum(n * sub - 1, 0), r))
    shape = jax.ShapeDtypeStruct((b, pairs, length, dil * LANES), F32)
    o, lse = pl.pallas_call(
        _dilated_kernel,
        grid=(b, pairs, dil, length // tq),
        in_specs=[own, prev, own, prev, own],
        out_specs=[own, own],
        out_shape=[shape, shape],
        compiler_params=pltpu.CompilerParams(
            dimension_semantics=("arbitrary",) * 4, vmem_limit_bytes=VMEM_LIMIT),
        name=f"dilated_{dil}",
    )(view(q_hm), view(k_hm), view(k_hm), view(v_hm), view(v_hm))
    return o.reshape(b, pairs, s, LANES), lse.reshape(b, pairs, s, LANES)


def _final_kernel(h_ref, y_ref, rt_ref, g_ref, o_ref):
    o_ref[...] = _rms(_moe_combine(h_ref, y_ref, rt_ref), g_ref[...])


def _final(h, y2, route, gain):
    n = h.shape[0]
    tm = PROJ_ROWS
    return pl.pallas_call(
        _final_kernel,
        grid=(n // tm,),
        in_specs=[pl.BlockSpec((tm, D_MODEL), lambda i: (i, 0)),
                  pl.BlockSpec((tm, MOE_TOPK * D_MODEL), lambda i: (i, 0)),
                  pl.BlockSpec((tm, ROUTER_WIDTH), lambda i: (i, 0)),
                  pl.BlockSpec((1, D_MODEL), lambda i: (0, 0))],
        out_specs=pl.BlockSpec((tm, D_MODEL), lambda i: (i, 0)),
        out_shape=jax.ShapeDtypeStruct((n, D_MODEL), F32),
        compiler_params=pltpu.CompilerParams(
            dimension_semantics=("arbitrary",), vmem_limit_bytes=VMEM_LIMIT),
        name="final_norm",
    )(h, y2, route, gain.reshape(1, D_MODEL))


def _router_params(w_group, w_fine, b_group, b_fine):
    w = jnp.concatenate([w_group, w_fine], axis=-1).astype(F32)
    pad = ROUTER_WIDTH - w.shape[1]
    w = jnp.pad(w, ((0, 0), (0, pad)))
    bias = jnp.pad(jnp.concatenate([b_group, b_fine]).astype(F32), (0, pad)).reshape(1, ROUTER_WIDTH)
    w_hi = w.astype(BF16)
    return w_hi, (w - w_hi.astype(F32)).astype(BF16), bias


def kernel(x, mix_norm_even, w_in_even, w_out_even, mix_norm_odd, w_in_odd, w_out_odd, ffn_norm,
           w_router_group, b_router_group, w_router_expert, b_router_expert,
           w_expert_gate, w_expert_up, w_expert_down, final_norm):
    batch, seq, d = x.shape
    n = batch * seq
    x2 = x.reshape(n, d)

    qkv, rg, kmean, kst, vst = _proj_even(x2, mix_norm_even[0], w_in_even[0].astype(BF16),
                                          batch, seq)
    qkv3 = qkv.reshape(batch, seq, EVEN_QKV_WIDTH)
    a_out = _moba(qkv3, kmean.reshape(batch, seq // MOBA_BLOCK, MOBA_WIDTH), kst, vst)
    r_out = _retention(qkv3, rg.reshape(batch, seq, RET_V_WIDTH))
    router = _router_params(w_router_group[0], w_router_expert[0],
                            b_router_group[0], b_router_expert[0])
    h1, xn, route = _outproj_even(x2, a_out.reshape(n, MOBA_WIDTH), r_out.reshape(n, RET_V_WIDTH),
                                  w_out_even[0].astype(BF16), ffn_norm[0], router)
    y = _moe(xn, route, w_expert_gate[0].astype(BF16), w_expert_up[0].astype(BF16),
             w_expert_down[0].astype(BF16))

    h2, q_hm, k_hm, v_hm = _proj_odd(h1, y, route, mix_norm_odd[0], w_in_odd[0].astype(BF16),
                                     batch, seq)
    branches = [_dilated_branch(q_hm, k_hm, v_hm, dil) for (_, dil) in DIL_PATTERNS]
    router = _router_params(w_router_group[1], w_router_expert[1],
                            b_router_group[1], b_router_expert[1])
    h3, xn, route = _outproj_odd(h2, branches, w_out_odd[0].astype(BF16), ffn_norm[1], router, seq)
    y = _moe(xn, route, w_expert_gate[1].astype(BF16), w_expert_up[1].astype(BF16),
             w_expert_down[1].astype(BF16))
    return _final(h3, y, route, final_norm).reshape(batch, seq, d)
```

```python
import functools

import jax
import jax.numpy as jnp
from jax import lax
from jax.experimental import pallas as pl
from jax.experimental.pallas import tpu as pltpu

F32 = jnp.float32
BF16 = jnp.bfloat16

D_MODEL = 1024
HEAD_DIM = 64
ROPE_THETA = 10000.0
RET_THETA = 10000.0
NORM_EPS = 1e-6

MOBA_HEADS = 8
MOBA_BLOCK = 256
MOBA_TOPK = 3
MOBA_CHUNK = 16
MOBA_WIDTH = MOBA_HEADS * HEAD_DIM
RET_HEADS = 4
RET_QK_DIM = 64
RET_V_DIM = 128
RET_CHUNK = 128
RET_QK_WIDTH = RET_HEADS * RET_QK_DIM
RET_V_WIDTH = RET_HEADS * RET_V_DIM
EVEN_IN_WIDTH = 3 * MOBA_WIDTH + 2 * RET_QK_WIDTH + 2 * RET_V_WIDTH
EVEN_QKV_WIDTH = EVEN_IN_WIDTH - RET_V_WIDTH

DIL_HEADS = D_MODEL // HEAD_DIM
DIL_PATTERNS = ((128, 1), (512, 4), (2048, 16))
DIL_BLOCK = 128
ODD_IN_WIDTH = 3 * D_MODEL

MOE_GROUPS = 4
MOE_EXPERTS_PER_GROUP = 8
MOE_EXPERTS = MOE_GROUPS * MOE_EXPERTS_PER_GROUP
MOE_TOPK = 2
MOE_HIDDEN = D_MODEL // 2

LANES = 128
ROUTER_WIDTH = LANES
PROJ_ROWS = 512
EXPERT_ROWS = 256
DIL_ROWS = 512
RET_ROWS = 512
VMEM_LIMIT = 56 * 1024 * 1024
NEG_INF = float("-inf")


def _dot(a, b):
    return jnp.dot(a, b, preferred_element_type=F32)


def _dot_nt(a, b):
    return lax.dot_general(a, b, (((1,), (1,)), ((), ())), preferred_element_type=F32)


def _dot_tn(a, b):
    return lax.dot_general(a, b, (((0,), (0,)), ((), ())), preferred_element_type=F32)


def _rms(x, gain):
    ms = jnp.mean(x * x, axis=-1, keepdims=True)
    return x * lax.rsqrt(ms + NORM_EPS) * gain


def _swap_half(a, first_half):
    return jnp.where(first_half, pltpu.roll(a, LANES - 32, 1), pltpu.roll(a, 32, 1))


def _swap_pairs(a, even):
    return jnp.where(even, pltpu.roll(a, LANES - 1, 1), pltpu.roll(a, 1, 1))


def _rotary_tables(s):
    half = HEAD_DIM // 2
    inv_freq = 1.0 / (ROPE_THETA ** (jnp.arange(half, dtype=F32) * (2.0 / HEAD_DIM)))
    ang = jnp.arange(s, dtype=F32)[:, None] * inv_freq[None, :]
    cos, sin = jnp.cos(ang), jnp.sin(ang)
    return (jnp.concatenate([cos, cos, cos, cos], axis=-1),
            jnp.concatenate([-sin, sin, -sin, sin], axis=-1))


def _retnet_tables(s):
    half = RET_QK_DIM // 2
    freq = 1.0 / (RET_THETA ** jnp.linspace(0.0, 1.0, half, dtype=F32))
    ang = jnp.arange(s, dtype=F32)[:, None] * freq[None, :]
    cos, sin = jnp.cos(ang), jnp.sin(ang)
    cos2 = jnp.repeat(cos, 2, axis=-1)
    sin2 = jnp.stack([-sin, sin], axis=-1).reshape(s, RET_QK_DIM)
    return jnp.tile(cos2, (1, 2)), jnp.tile(sin2, (1, 2))


def _retention_tables():
    c = RET_CHUNK
    log_gamma = jnp.log(1.0 - 2.0 ** (-5.0 - jnp.arange(RET_HEADS, dtype=F32)))
    pos = jnp.arange(c, dtype=F32)
    rel = pos[:, None] - pos[None, :]
    decay = jnp.where(rel >= 0, jnp.exp(log_gamma[:, None, None] * jnp.maximum(rel, 0.0)), 0.0)
    tail = jnp.exp(log_gamma[:, None] * (c - 1.0 - pos))[:, :, None]
    head = jnp.exp(log_gamma[:, None] * (pos + 1.0))[:, :, None]
    gamma_c = jnp.exp(log_gamma * c)[:, None, None]
    return decay, tail, head, gamma_c


def _stack_blocks(a, lower_lanes):
    top, bot = a[:MOBA_BLOCK], a[MOBA_BLOCK:]
    head_a = jnp.where(lower_lanes, top, pltpu.roll(bot, HEAD_DIM, 1))
    head_b = jnp.where(lower_lanes, pltpu.roll(top, HEAD_DIM, 1), bot)
    return head_a.astype(BF16), head_b.astype(BF16)


def _proj_even_kernel(x_ref, g_ref, w_ref, rc_ref, rs_ref, tc_ref, ts_ref,
                      qkv_ref, rg_ref, km_ref, kst_ref, vst_ref):
    tm = x_ref.shape[0]
    xn = _rms(x_ref[...], g_ref[...]).astype(BF16)
    lane = lax.broadcasted_iota(jnp.int32, (tm, LANES), 1)
    first_half = (lane % HEAD_DIM) < (HEAD_DIM // 2)
    even = (lane % 2) == 0
    lower_lanes = lax.broadcasted_iota(jnp.int32, (MOBA_BLOCK, LANES), 1) < HEAD_DIM
    rc, rs, tc, ts = rc_ref[...], rs_ref[...], tc_ref[...], ts_ref[...]
    scale = HEAD_DIM ** -0.5
    n_chunk = 4
    for j in range(EVEN_IN_WIDTH // (n_chunk * LANES)):
        acc = _dot(xn, w_ref[:, j * n_chunk * LANES:(j + 1) * n_chunk * LANES])
        for u in range(n_chunk):
            c = j * n_chunk + u
            a = acc[:, u * LANES:(u + 1) * LANES]
            cols = slice(c * LANES, (c + 1) * LANES)
            if c < 4:
                a = (a * rc + _swap_half(a, first_half) * rs) * scale
            elif c < 8:
                a = a * rc + _swap_half(a, first_half) * rs
                km_ref[0, :, (c - 4) * LANES:(c - 3) * LANES] = jnp.mean(
                    a.reshape(tm // MOBA_BLOCK, MOBA_BLOCK, LANES), axis=1)
                kst_ref[2 * (c - 4)], kst_ref[2 * (c - 4) + 1] = _stack_blocks(a, lower_lanes)
            elif c < 12:
                vst_ref[2 * (c - 8)], vst_ref[2 * (c - 8) + 1] = _stack_blocks(a, lower_lanes)
            elif c < 14:
                a = a * tc + _swap_pairs(a, even) * ts
            elif c < 16:
                a = (a * tc + _swap_pairs(a, even) * ts) * (RET_QK_DIM ** -0.5)
            elif c < 20:
                pass
            else:
                rg_ref[:, (c - 20) * LANES:(c - 19) * LANES] = a
                continue
            qkv_ref[:, cols] = a.astype(BF16)


def _proj_even(x2, gain, w_bf, batch, seq):
    n = x2.shape[0]
    tm = PROJ_ROWS
    assert tm == 2 * MOBA_BLOCK and seq % (MOBA_CHUNK * MOBA_BLOCK) == 0
    rc, rs = _rotary_tables(seq)
    tc, ts = _retnet_tables(seq)
    pos_blocks = seq // tm
    nblk = seq // MOBA_BLOCK
    tab = pl.BlockSpec((tm, LANES), lambda i: (i % pos_blocks, 0))
    tiles_per_chunk = MOBA_CHUNK // 2
    kst_spec = pl.BlockSpec(
        (None, MOBA_HEADS, None, MOBA_BLOCK, LANES),
        lambda i: (i // pos_blocks, 0, (i % pos_blocks) // tiles_per_chunk, 0,
                   (i % pos_blocks) % tiles_per_chunk))
    vst_spec = pl.BlockSpec(
        (None, MOBA_HEADS, None, MOBA_BLOCK, LANES),
        lambda i: (i // pos_blocks, 0, (i % pos_blocks) // 2, 0, (i % pos_blocks) % 2))
    return pl.pallas_call(
        _proj_even_kernel,
        grid=(n // tm,),
        in_specs=[
            pl.BlockSpec((tm, D_MODEL), lambda i: (i, 0)),
            pl.BlockSpec((1, D_MODEL), lambda i: (0, 0)),
            pl.BlockSpec((D_MODEL, EVEN_IN_WIDTH), lambda i: (0, 0)),
            tab, tab, tab, tab,
        ],
        out_specs=[
            pl.BlockSpec((tm, EVEN_QKV_WIDTH), lambda i: (i, 0)),
            pl.BlockSpec((tm, RET_V_WIDTH), lambda i: (i, 0)),
            pl.BlockSpec((1, tm // MOBA_BLOCK, MOBA_WIDTH), lambda i: (i, 0, 0)),
            kst_spec, vst_spec,
        ],
        out_shape=[
            jax.ShapeDtypeStruct((n, EVEN_QKV_WIDTH), BF16),
            jax.ShapeDtypeStruct((n, RET_V_WIDTH), F32),
            jax.ShapeDtypeStruct((n // tm, tm // MOBA_BLOCK, MOBA_WIDTH), F32),
            jax.ShapeDtypeStruct((batch, MOBA_HEADS, nblk // MOBA_CHUNK, MOBA_BLOCK,
                                  MOBA_CHUNK * HEAD_DIM), BF16),
            jax.ShapeDtypeStruct((batch, MOBA_HEADS, nblk // 4, MOBA_BLOCK, 4 * HEAD_DIM), BF16),
        ],
        compiler_params=pltpu.CompilerParams(
            dimension_semantics=("arbitrary",), vmem_limit_bytes=VMEM_LIMIT),
        name="proj_even",
    )(x2, gain.reshape(1, D_MODEL), w_bf, rc, rs, tc, ts)


def _moba_kernel(q_ref, ko_ref, vo_ref, km_ref, kst_ref, vst_ref, o_ref):
    i = pl.program_id(2)
    nblk = km_ref.shape[0]
    bq = MOBA_BLOCK
    kw = MOBA_CHUNK * HEAD_DIM
    n_chunks = (i + MOBA_CHUNK - 1) // MOBA_CHUNK
    blk = lax.broadcasted_iota(jnp.int32, (bq, nblk), 1)
    blk_f = blk.astype(F32)
    row = lax.broadcasted_iota(jnp.int32, (bq, bq), 0)
    col = lax.broadcasted_iota(jnp.int32, (bq, bq), 1)
    lane_kw = lax.broadcasted_iota(jnp.int32, (1, kw), 1)
    lane_blk = (lane_kw // HEAD_DIM).astype(F32)
    lane_grp = (lane_kw // MOBA_BLOCK).astype(F32)
    lane_sub = lax.broadcasted_iota(jnp.int32, (1, bq), 1) // HEAD_DIM
    outs = []
    for h in range(2):
        sl = slice(h * HEAD_DIM, (h + 1) * HEAD_DIM)
        q = q_ref[:, sl]
        km = km_ref[:, sl]
        km_hi = km.astype(BF16)
        km_lo = (km - km_hi.astype(F32)).astype(BF16)
        gate = _dot_nt(q, km_hi) + _dot_nt(q, km_lo)
        g = jnp.where(blk < i, gate, NEG_INF)
        sels = []
        for _ in range(MOBA_TOPK):
            m = jnp.max(g, axis=-1, keepdims=True)
            cand = (g == m) & (m > NEG_INF)
            idx = jnp.min(jnp.where(cand, blk_f, float(nblk)), axis=-1, keepdims=True)
            g = jnp.where(blk_f == idx, NEG_INF, g)
            sels.append(jnp.where(idx < nblk, idx, -1.0))
        sel_f = jnp.concatenate(sels, axis=0)
        sel = sel_f.astype(jnp.int32)

        s_own = jnp.where(col <= row, _dot_nt(q, ko_ref[:, sl]), NEG_INF)

        q2 = jnp.concatenate([q, q], axis=1)
        qt = jnp.tile(jnp.concatenate([q2] * MOBA_TOPK, axis=0), (1, kw // LANES))
        sel_blk = sel_f.astype(BF16)

        def scores(c, acc, h=h, qt=qt, sel_blk=sel_blk):
            ids = (lane_blk + (c * MOBA_CHUNK).astype(F32)).astype(BF16)
            lhs = jnp.where(sel_blk == ids, qt, jnp.zeros_like(qt))
            return acc + _dot_nt(lhs, kst_ref[h, c])

        s_sel = lax.fori_loop(0, n_chunks, scores, jnp.zeros((MOBA_TOPK * bq, bq), F32))
        s_sel = jnp.where(sel >= 0, s_sel, NEG_INF)
        m = jnp.max(s_own, axis=-1, keepdims=True)
        for r in range(MOBA_TOPK):
            m = jnp.maximum(m, jnp.max(s_sel[r * bq:(r + 1) * bq], axis=-1, keepdims=True))
        p_own = jnp.exp(s_own - m)
        p_sel = jnp.exp(s_sel - jnp.concatenate([m] * MOBA_TOPK, axis=0))
        den = jnp.sum(p_own, axis=-1, keepdims=True)
        for r in range(MOBA_TOPK):
            den = den + jnp.sum(p_sel[r * bq:(r + 1) * bq], axis=-1, keepdims=True)

        pt = jnp.tile(p_sel.astype(BF16), (1, kw // bq))
        sel_grp = lax.shift_right_arithmetic(sel, 2).astype(F32).astype(BF16)

        def values(c, acc, h=h, pt=pt, sel_grp=sel_grp):
            ids = (lane_grp + (c * (MOBA_CHUNK // 4)).astype(F32)).astype(BF16)
            lhs = jnp.where(sel_grp == ids, pt, jnp.zeros_like(pt))
            return acc + _dot(lhs, vst_ref[h, c])

        o4 = lax.fori_loop(0, n_chunks, values, jnp.zeros((MOBA_TOPK * bq, bq), F32))
        o4 = jnp.where(lane_sub == (sel & 3), o4, 0.0)
        x = o4[:bq]
        for r in range(1, MOBA_TOPK):
            x = x + o4[r * bq:(r + 1) * bq]
        y = x[:, :LANES] + x[:, LANES:]
        y = y + pltpu.roll(y, HEAD_DIM, 1)
        outs.append((_dot(p_own.astype(BF16), vo_ref[:, sl]) + y[:, :HEAD_DIM]) / den)
    o_ref[...] = jnp.concatenate(outs, axis=-1).astype(BF16)


def _moba(qkv3, kmean3, kst, vst):
    b, s, _ = qkv3.shape
    nblk = s // MOBA_BLOCK
    pairs = MOBA_HEADS // 2
    n_chunks = nblk // MOBA_CHUNK
    kw = MOBA_CHUNK * HEAD_DIM
    own = lambda t: pl.BlockSpec((None, MOBA_BLOCK, LANES), lambda bi, hp, i: (bi, i, t * pairs + hp))
    return pl.pallas_call(
        _moba_kernel,
        grid=(b, pairs, nblk),
        in_specs=[
            own(0), own(1), own(2),
            pl.BlockSpec((None, nblk, LANES), lambda bi, hp, i: (bi, 0, hp)),
            pl.BlockSpec((None, 2, n_chunks, MOBA_BLOCK, kw), lambda bi, hp, i: (bi, hp, 0, 0, 0)),
            pl.BlockSpec((None, 2, n_chunks, kw, MOBA_BLOCK), lambda bi, hp, i: (bi, hp, 0, 0, 0)),
        ],
        out_specs=own(0),
        out_shape=jax.ShapeDtypeStruct((b, s, MOBA_WIDTH), BF16),
        compiler_params=pltpu.CompilerParams(
            dimension_semantics=("arbitrary", "arbitrary", "arbitrary"),
            vmem_limit_bytes=VMEM_LIMIT),
        name="moba",
    )(qkv3, qkv3, qkv3, kmean3, kst, vst.reshape(b, MOBA_HEADS, n_chunks, kw, MOBA_BLOCK))


def _retention_kernel(q_ref, k_ref, v_ref, g_ref, dec_ref, tail_ref, head_ref, gc_ref,
                      o_ref, state_ref):
    @pl.when(pl.program_id(2) == 0)
    def _():
        state_ref[...] = jnp.zeros_like(state_ref)

    c = RET_CHUNK
    for h in range(2):
        qk = slice(h * RET_QK_DIM, (h + 1) * RET_QK_DIM)
        vv = slice(h * RET_V_DIM, (h + 1) * RET_V_DIM)
        dec, tail, head, gc = dec_ref[h], tail_ref[h], head_ref[h], gc_ref[h]
        for u in range(q_ref.shape[0] // c):
            rows = slice(u * c, (u + 1) * c)
            q, k, v = q_ref[rows, qk], k_ref[rows, qk], v_ref[rows, vv]
            scores = _dot_nt(q, k) * dec
            state = state_ref[h]
            y = (_dot(scores.astype(BF16), v)
                 + _dot((q.astype(F32) * head).astype(BF16), state.astype(BF16)))
            state_ref[h] = gc * state + _dot_tn((k.astype(F32) * tail).astype(BF16), v)
            mu = jnp.mean(y, axis=-1, keepdims=True)
            yc = y - mu
            var = jnp.mean(yc * yc, axis=-1, keepdims=True)
            gate = g_ref[rows, vv]
            swish = gate / (1.0 + jnp.exp(-gate))
            o_ref[rows, vv] = (yc * lax.rsqrt(var + NORM_EPS) * swish).astype(BF16)


def _retention(qkv3, rg3):
    b, s, _ = qkv3.shape
    t = RET_ROWS
    pairs = RET_HEADS // 2
    dec, tail, head, gc = _retention_tables()
    q0 = 3 * MOBA_WIDTH // LANES
    k0 = q0 + RET_QK_WIDTH // LANES
    v0 = (3 * MOBA_WIDTH + 2 * RET_QK_WIDTH) // (2 * RET_V_DIM)
    return pl.pallas_call(
        _retention_kernel,
        grid=(b, pairs, s // t),
        in_specs=[
            pl.BlockSpec((None, t, LANES), lambda bi, hp, n: (bi, n, q0 + hp)),
            pl.BlockSpec((None, t, LANES), lambda bi, hp, n: (bi, n, k0 + hp)),
            pl.BlockSpec((None, t, 2 * RET_V_DIM), lambda bi, hp, n: (bi, n, v0 + hp)),
            pl.BlockSpec((None, t, 2 * RET_V_DIM), lambda bi, hp, n: (bi, n, hp)),
            pl.BlockSpec((2, RET_CHUNK, RET_CHUNK), lambda bi, hp, n: (hp, 0, 0)),
            pl.BlockSpec((2, RET_CHUNK, 1), lambda bi, hp, n: (hp, 0, 0)),
            pl.BlockSpec((2, RET_CHUNK, 1), lambda bi, hp, n: (hp, 0, 0)),
            pl.BlockSpec((2, 1, 1), lambda bi, hp, n: (hp, 0, 0)),
        ],
        out_specs=pl.BlockSpec((None, t, 2 * RET_V_DIM), lambda bi, hp, n: (bi, n, hp)),
        out_shape=jax.ShapeDtypeStruct((b, s, RET_V_WIDTH), BF16),
        scratch_shapes=[pltpu.VMEM((2, RET_QK_DIM, RET_V_DIM), F32)],
        compiler_params=pltpu.CompilerParams(
            dimension_semantics=("arbitrary", "arbitrary", "arbitrary"),
            vmem_limit_bytes=VMEM_LIMIT),
        name="retention",
    )(qkv3, qkv3, qkv3, rg3, dec, tail, head, gc)


def _route(xn, wr_hi_ref, wr_lo_ref, br_ref):
    x_hi = xn.astype(BF16)
    x_lo = (xn - x_hi.astype(F32)).astype(BF16)
    w_hi = wr_hi_ref[...]
    z = _dot(x_hi, w_hi) + _dot(x_lo, w_hi) + _dot(x_hi, wr_lo_ref[...]) + br_ref[...]
    lane = lax.broadcasted_iota(jnp.int32, z.shape, 1)

    def top1(v):
        m = jnp.max(v, axis=-1, keepdims=True)
        return m, jnp.min(jnp.where(v == m, lane, ROUTER_WIDTH), axis=-1, keepdims=True)

    g_logit = jnp.where(lane < MOE_GROUPS, z, NEG_INF)
    g_max, grp = top1(g_logit)
    g_val = 1.0 / jnp.sum(jnp.exp(g_logit - g_max), axis=-1, keepdims=True)
    lo = MOE_GROUPS + MOE_EXPERTS_PER_GROUP * grp
    f = jnp.where((lane >= lo) & (lane < lo + MOE_EXPERTS_PER_GROUP), z, NEG_INF)
    v1, i1 = top1(f)
    v2, i2 = top1(jnp.where(lane == i1, NEG_INF, f))
    t = jnp.exp(v2 - v1)
    g1 = g_val / (1.0 + t)
    g2 = g_val * t / (1.0 + t)
    e1 = (i1 - MOE_GROUPS).astype(F32)
    e2 = (i2 - MOE_GROUPS).astype(F32)
    return jnp.where(lane == 0, e1, jnp.where(lane == 1, e2, jnp.where(
        lane == 2, g1, jnp.where(lane == 3, g2, 0.0))))


def _outproj_even_kernel(x_ref, a_ref, r_ref, wo_ref, gn_ref, wr_hi_ref, wr_lo_ref, br_ref,
                         h_ref, xn_ref, rt_ref):
    h = x_ref[...] + (_dot(a_ref[...], wo_ref[:MOBA_WIDTH, :])
                      + _dot(r_ref[...], wo_ref[MOBA_WIDTH:, :]))
    h_ref[...] = h
    xn = _rms(h, gn_ref[...])
    xn_ref[...] = xn
    rt_ref[...] = _route(xn, wr_hi_ref, wr_lo_ref, br_ref)


def _outproj_even(x2, a2, r2, wo_bf, gain, router):
    n = x2.shape[0]
    tm = PROJ_ROWS
    row = lambda w: pl.BlockSpec((tm, w), lambda i: (i, 0))
    full = lambda r, w: pl.BlockSpec((r, w), lambda i: (0, 0))
    return pl.pallas_call(
        _outproj_even_kernel,
        grid=(n // tm,),
        in_specs=[row(D_MODEL), row(MOBA_WIDTH), row(RET_V_WIDTH), full(D_MODEL, D_MODEL),
                  full(1, D_MODEL), full(D_MODEL, ROUTER_WIDTH), full(D_MODEL, ROUTER_WIDTH),
                  full(1, ROUTER_WIDTH)],
        out_specs=[row(D_MODEL), row(D_MODEL), row(ROUTER_WIDTH)],
        out_shape=[jax.ShapeDtypeStruct((n, D_MODEL), F32),
                   jax.ShapeDtypeStruct((n, D_MODEL), F32),
                   jax.ShapeDtypeStruct((n, ROUTER_WIDTH), F32)],
        compiler_params=pltpu.CompilerParams(
            dimension_semantics=("arbitrary",), vmem_limit_bytes=VMEM_LIMIT),
        name="outproj_even",
    )(x2, a2, r2, wo_bf, gain.reshape(1, D_MODEL), *router)


def _outproj_odd_kernel(h_ref, o1_ref, o2_ref, o3_ref, l1_ref, l2_ref, l3_ref, wo_ref,
                        gn_ref, wr_hi_ref, wr_lo_ref, br_ref, hout_ref, xn_ref, rt_ref):
    pieces = []
    for hp in range(DIL_HEADS // 2):
        la, lb, lc = l1_ref[hp], l2_ref[hp], l3_ref[hp]
        m = jnp.maximum(jnp.maximum(la, lb), lc)
        ea, eb, ec = jnp.exp(la - m), jnp.exp(lb - m), jnp.exp(lc - m)
        mix = (ea * o1_ref[hp] + eb * o2_ref[hp] + ec * o3_ref[hp]) / (ea + eb + ec)
        pieces.append(mix.astype(BF16))
    h = h_ref[...] + _dot(jnp.concatenate(pieces, axis=-1), wo_ref[...])
    hout_ref[...] = h
    xn = _rms(h, gn_ref[...])
    xn_ref[...] = xn
    rt_ref[...] = _route(xn, wr_hi_ref, wr_lo_ref, br_ref)


def _outproj_odd(h2, branches, wo_bf, gain, router, seq):
    n = h2.shape[0]
    tm = PROJ_ROWS
    pairs = DIL_HEADS // 2
    per_seq = seq // tm
    row = lambda w: pl.BlockSpec((tm, w), lambda i: (i, 0))
    full = lambda r, w: pl.BlockSpec((r, w), lambda i: (0, 0))
    hm = pl.BlockSpec((None, pairs, tm, LANES), lambda i: (i // per_seq, 0, i % per_seq, 0))
    outs = [o for (o, _) in branches]
    lses = [l for (_, l) in branches]
    return pl.pallas_call(
        _outproj_odd_kernel,
        grid=(n // tm,),
        in_specs=[row(D_MODEL), hm, hm, hm, hm, hm, hm, full(D_MODEL, D_MODEL),
                  full(1, D_MODEL), full(D_MODEL, ROUTER_WIDTH), full(D_MODEL, ROUTER_WIDTH),
                  full(1, ROUTER_WIDTH)],
        out_specs=[row(D_MODEL), row(D_MODEL), row(ROUTER_WIDTH)],
        out_shape=[jax.ShapeDtypeStruct((n, D_MODEL), F32),
                   jax.ShapeDtypeStruct((n, D_MODEL), F32),
                   jax.ShapeDtypeStruct((n, ROUTER_WIDTH), F32)],
        compiler_params=pltpu.CompilerParams(
            dimension_semantics=("arbitrary",), vmem_limit_bytes=VMEM_LIMIT),
        name="outproj_odd",
    )(h2, *outs, *lses, wo_bf, gain.reshape(1, D_MODEL), *router)


def _plan_rows(expert):
    rb = EXPERT_ROWS
    n_assign = expert.size
    e_flat = expert.T.reshape(-1)
    order = jnp.argsort(e_flat).astype(jnp.int32)
    e_s = e_flat[order]
    counts = jnp.bincount(e_flat, length=MOE_EXPERTS).astype(jnp.int32)
    starts = jnp.cumsum(counts) - counts
    padded = ((counts + rb - 1) // rb) * rb
    p_ends = jnp.cumsum(padded)
    p_starts = p_ends - padded
    dest = p_starts[e_s] + (jnp.arange(n_assign, dtype=jnp.int32) - starts[e_s])
    n_blocks = n_assign // rb + MOE_EXPERTS
    n_rows = n_blocks * rb
    row_dst = jnp.zeros((n_rows,), jnp.int32).at[dest].set(order)
    blk_start = jnp.arange(n_blocks, dtype=jnp.int32) * rb
    blk_exp = jnp.minimum(jnp.sum(p_ends[None, :] <= blk_start[:, None], axis=1),
                          MOE_EXPERTS - 1).astype(jnp.int32)
    blk_valid = jnp.clip(counts[blk_exp] - (blk_start - p_starts[blk_exp]), 0, rb).astype(jnp.int32)
    n_used = (p_ends[-1] // rb).astype(jnp.int32).reshape(1)
    return blk_exp, blk_valid, row_dst, n_used


def _expert_kernel(blk_exp, blk_valid, row_dst, n_used,
                   wg_ref, wu_ref, wd_ref, x_hbm, out_hbm,
                   xbuf, ybuf, gsem, ssem):
    del blk_exp
    rb = EXPERT_ROWS
    n_tok = x_hbm.shape[0]
    i = pl.program_id(0)
    used = n_used[0]
    slot = i % 2

    def gather_copy(tok, r, s):
        return pltpu.make_async_copy(x_hbm.at[pl.ds(tok, 1)], xbuf.at[s, pl.ds(r, 1)], gsem.at[s])

    def scatter_copy(dst, r, s):
        return pltpu.make_async_copy(ybuf.at[s, pl.ds(r, 1)], out_hbm.at[pl.ds(dst, 1)], ssem.at[s])

    def start_gather(b, s):
        def body(r, carry):
            a = row_dst[b * rb + r]
            gather_copy(a - jnp.where(a >= n_tok, n_tok, 0), r, s).start()
            return carry
        lax.fori_loop(0, rb, body, 0, unroll=8)

    def wait_scatter(count, s):
        wide = 16

        def wide_body(r, carry):
            pltpu.make_async_copy(ybuf.at[s, pl.ds(0, wide)], out_hbm.at[pl.ds(0, wide)],
                                  ssem.at[s]).wait()
            return carry
        lax.fori_loop(0, count // wide, wide_body, 0)

        def body(r, carry):
            scatter_copy(0, 0, s).wait()
            return carry
        lax.fori_loop(0, count % wide, body, 0)

    @pl.when(i == 0)
    def _():
        start_gather(0, 0)

    @pl.when(i < used)
    def _():
        @pl.when(i + 1 < used)
        def _():
            start_gather(i + 1, 1 - slot)

        pltpu.make_async_copy(x_hbm.at[pl.ds(0, rb)], xbuf.at[slot], gsem.at[slot]).wait()

        @pl.when(i >= 2)
        def _():
            wait_scatter(blk_valid[i - 2], slot)

        xb = xbuf[slot].astype(BF16)
        g = _dot(xb, wg_ref[...])
        u = _dot(xb, wu_ref[...])
        hid = (g / (1.0 + jnp.exp(-g)) * u).astype(BF16)
        ybuf[slot] = _dot(hid, wd_ref[...])

        def scatter_body(r, carry):
            scatter_copy(row_dst[i * rb + r], r, slot).start()
            return carry
        lax.fori_loop(0, blk_valid[i], scatter_body, 0)

        @pl.when(i == used - 1)
        def _():
            wait_scatter(blk_valid[i], slot)

            @pl.when(i >= 1)
            def _():
                wait_scatter(blk_valid[i - 1], 1 - slot)


def _experts(xn, plan, wg_bf, wu_bf, wd_bf):
    blk_exp, blk_valid, row_dst, n_used = plan
    n = xn.shape[0]
    rb = EXPERT_ROWS
    n_blocks = blk_exp.shape[0]
    w_in = pl.BlockSpec((None, D_MODEL, MOE_HIDDEN), lambda i, be, bv, rd, nu: (be[i], 0, 0))
    w_out = pl.BlockSpec((None, MOE_HIDDEN, D_MODEL), lambda i, be, bv, rd, nu: (be[i], 0, 0))
    return pl.pallas_call(
        _expert_kernel,
        grid_spec=pltpu.PrefetchScalarGridSpec(
            num_scalar_prefetch=4,
            grid=(n_blocks,),
            in_specs=[
                w_in, w_in, w_out,
                pl.BlockSpec(memory_space=pl.ANY),
            ],
            out_specs=pl.BlockSpec(memory_space=pl.ANY),
            scratch_shapes=[
                pltpu.VMEM((2, rb, D_MODEL), F32),
                pltpu.VMEM((2, rb, D_MODEL), F32),
                pltpu.SemaphoreType.DMA((2,)),
                pltpu.SemaphoreType.DMA((2,)),
            ],
        ),
        out_shape=jax.ShapeDtypeStruct((MOE_TOPK * n, D_MODEL), F32),
        compiler_params=pltpu.CompilerParams(
            dimension_semantics=("arbitrary",), vmem_limit_bytes=VMEM_LIMIT),
        name="experts",
    )(blk_exp, blk_valid, row_dst, n_used, wg_bf, wu_bf, wd_bf, xn)


def _moe(xn, route, wg_bf, wu_bf, wd_bf):
    expert = route[:, :MOE_TOPK].astype(jnp.int32)
    return _experts(xn, _plan_rows(expert), wg_bf, wu_bf, wd_bf)


def _moe_combine(h_ref, y0_ref, y1_ref, rt_ref):
    rt = rt_ref[...]
    return h_ref[...] + (y0_ref[...] * rt[:, 2:3] + y1_ref[...] * rt[:, 3:4])


def _moe_output_specs(n, tm):
    return [pl.BlockSpec((tm, D_MODEL), lambda i: (i, 0)),
            pl.BlockSpec((tm, D_MODEL), lambda i: (n // tm + i, 0))]


def _proj_odd_kernel(h_ref, y0_ref, y1_ref, rt_ref, g_ref, w_ref, rc_ref, rs_ref,
                     hout_ref, q_ref, k_ref, v_ref):
    tm = h_ref.shape[0]
    h = _moe_combine(h_ref, y0_ref, y1_ref, rt_ref)
    hout_ref[...] = h
    xn = _rms(h, g_ref[...]).astype(BF16)
    lane = lax.broadcasted_iota(jnp.int32, (tm, LANES), 1)
    first_half = (lane % HEAD_DIM) < (HEAD_DIM // 2)
    rc, rs = rc_ref[...], rs_ref[...]
    scale = HEAD_DIM ** -0.5
    n_chunk = 4
    pairs = DIL_HEADS // 2
    for j in range(ODD_IN_WIDTH // (n_chunk * LANES)):
        acc = _dot(xn, w_ref[:, j * n_chunk * LANES:(j + 1) * n_chunk * LANES])
        for u in range(n_chunk):
            c = j * n_chunk + u
            a = acc[:, u * LANES:(u + 1) * LANES]
            if c < pairs:
                q_ref[c] = ((a * rc + _swap_half(a, first_half) * rs) * scale).astype(BF16)
            elif c < 2 * pairs:
                k_ref[c - pairs] = (a * rc + _swap_half(a, first_half) * rs).astype(BF16)
            else:
                v_ref[c - 2 * pairs] = a.astype(BF16)


def _proj_odd(h1, y2, route, gain, w_bf, batch, seq):
    n = h1.shape[0]
    tm = PROJ_ROWS
    pairs = DIL_HEADS // 2
    rc, rs = _rotary_tables(seq)
    per_seq = seq // tm
    tab = pl.BlockSpec((tm, LANES), lambda i: (i % per_seq, 0))
    hm = pl.BlockSpec((None, pairs, tm, LANES), lambda i: (i // per_seq, 0, i % per_seq, 0))
    hm_shape = jax.ShapeDtypeStruct((batch, pairs, seq, LANES), BF16)
    return pl.pallas_call(
        _proj_odd_kernel,
        grid=(n // tm,),
        in_specs=[
            pl.BlockSpec((tm, D_MODEL), lambda i: (i, 0)),
            *_moe_output_specs(n, tm),
            pl.BlockSpec((tm, ROUTER_WIDTH), lambda i: (i, 0)),
            pl.BlockSpec((1, D_MODEL), lambda i: (0, 0)),
            pl.BlockSpec((D_MODEL, ODD_IN_WIDTH), lambda i: (0, 0)),
            tab, tab,
        ],
        out_specs=[pl.BlockSpec((tm, D_MODEL), lambda i: (i, 0)), hm, hm, hm],
        out_shape=[jax.ShapeDtypeStruct((n, D_MODEL), F32), hm_shape, hm_shape, hm_shape],
        compiler_params=pltpu.CompilerParams(
            dimension_semantics=("arbitrary",), vmem_limit_bytes=VMEM_LIMIT),
        name="proj_odd",
    )(h1, y2, y2, route, gain.reshape(1, D_MODEL), w_bf, rc, rs)


def _dilated_kernel(q_ref, kp_ref, ko_ref, vp_ref, vo_ref, o_ref, l_ref):
    n = pl.program_id(3)
    blk = DIL_BLOCK
    a = lax.broadcasted_iota(jnp.int32, (blk, 2 * blk), 0)
    e = lax.broadcasted_iota(jnp.int32, (blk, 2 * blk), 1)
    dist = blk + a - e
    band = (dist >= 0) & (dist <= blk)
    first = band & ((n > 0) | (e >= blk))
    for u in range(q_ref.shape[0] // blk):
        rows = slice(u * blk, (u + 1) * blk)
        o_parts, l_parts = [], []
        for h in range(2):
            sl = slice(h * HEAD_DIM, (h + 1) * HEAD_DIM)
            q = q_ref[rows, sl]
            if u == 0:
                k = jnp.concatenate([kp_ref[:, sl], ko_ref[:blk, sl]], axis=0)
                v = jnp.concatenate([vp_ref[:, sl], vo_ref[:blk, sl]], axis=0)
                mask = first
            else:
                k = ko_ref[(u - 1) * blk:(u + 1) * blk, sl]
                v = vo_ref[(u - 1) * blk:(u + 1) * blk, sl]
                mask = band
            s = jnp.where(mask, _dot_nt(q, k), NEG_INF)
            m = jnp.max(s, axis=-1, keepdims=True)
            p = jnp.exp(s - m)
            den = jnp.sum(p, axis=-1, keepdims=True)
            o_parts.append(_dot((p / den).astype(BF16), v))
            l_parts.append(jnp.broadcast_to(m + jnp.log(den), (blk, HEAD_DIM)))
        o_ref[rows, :] = jnp.concatenate(o_parts, axis=-1)
        l_ref[rows, :] = jnp.concatenate(l_parts, axis=-1)


def _dilated_branch(q_hm, k_hm, v_hm, dil):
    b, pairs, s, _ = q_hm.shape
    length = s // dil
    tq = min(DIL_ROWS, length)
    sub = tq // DIL_BLOCK
    view = lambda t: t.reshape(b, pairs, length, dil * LANES)
    own = pl.BlockSpec((None, None, tq, LANES), lambda bi, hp, r, n: (bi, hp, n, r))
    prev = pl.BlockSpec((None, None, DIL_BLOCK, LANES),
                        lambda bi, hp, r, n: (bi, hp, jnp.maximum(n * sub - 1, 0), r))
    shape = jax.ShapeDtypeStruct((b, pairs, length, dil * LANES), F32)
    o, lse = pl.pallas_call(
        _dilated_kernel,
        grid=(b, pairs, dil, length // tq),
        in_specs=[own, prev, own, prev, own],
        out_specs=[own, own],
        out_shape=[shape, shape],
        compiler_params=pltpu.CompilerParams(
            dimension_semantics=("arbitrary",) * 4, vmem_limit_bytes=VMEM_LIMIT),
        name=f"dilated_{dil}",
    )(view(q_hm), view(k_hm), view(k_hm), view(v_hm), view(v_hm))
    return o.reshape(b, pairs, s, LANES), lse.reshape(b, pairs, s, LANES)


def _final_kernel(h_ref, y0_ref, y1_ref, rt_ref, g_ref, o_ref):
    o_ref[...] = _rms(_moe_combine(h_ref, y0_ref, y1_ref, rt_ref), g_ref[...])


def _final(h, y2, route, gain):
    n = h.shape[0]
    tm = PROJ_ROWS
    return pl.pallas_call(
        _final_kernel,
        grid=(n // tm,),
        in_specs=[pl.BlockSpec((tm, D_MODEL), lambda i: (i, 0)),
                  *_moe_output_specs(n, tm),
                  pl.BlockSpec((tm, ROUTER_WIDTH), lambda i: (i, 0)),
                  pl.BlockSpec((1, D_MODEL), lambda i: (0, 0))],
        out_specs=pl.BlockSpec((tm, D_MODEL), lambda i: (i, 0)),
        out_shape=jax.ShapeDtypeStruct((n, D_MODEL), F32),
        compiler_params=pltpu.CompilerParams(
            dimension_semantics=("arbitrary",), vmem_limit_bytes=VMEM_LIMIT),
        name="final_norm",
    )(h, y2, y2, route, gain.reshape(1, D_MODEL))


def _router_params(w_group, w_fine, b_group, b_fine):
    w = jnp.concatenate([w_group, w_fine], axis=-1).astype(F32)
    pad = ROUTER_WIDTH - w.shape[1]
    w = jnp.pad(w, ((0, 0), (0, pad)))
    bias = jnp.pad(jnp.concatenate([b_group, b_fine]).astype(F32), (0, pad)).reshape(1, ROUTER_WIDTH)
    w_hi = w.astype(BF16)
    return w_hi, (w - w_hi.astype(F32)).astype(BF16), bias


def kernel(x, mix_norm_even, w_in_even, w_out_even, mix_norm_odd, w_in_odd, w_out_odd, ffn_norm,
           w_router_group, b_router_group, w_router_expert, b_router_expert,
           w_expert_gate, w_expert_up, w_expert_down, final_norm):
    batch, seq, d = x.shape
    n = batch * seq
    x2 = x.reshape(n, d)

    qkv, rg, kmean, kst, vst = _proj_even(x2, mix_norm_even[0], w_in_even[0].astype(BF16),
                                          batch, seq)
    qkv3 = qkv.reshape(batch, seq, EVEN_QKV_WIDTH)
    a_out = _moba(qkv3, kmean.reshape(batch, seq // MOBA_BLOCK, MOBA_WIDTH), kst, vst)
    r_out = _retention(qkv3, rg.reshape(batch, seq, RET_V_WIDTH))
    router = _router_params(w_router_group[0], w_router_expert[0],
                            b_router_group[0], b_router_expert[0])
    h1, xn, route = _outproj_even(x2, a_out.reshape(n, MOBA_WIDTH), r_out.reshape(n, RET_V_WIDTH),
                                  w_out_even[0].astype(BF16), ffn_norm[0], router)
    y = _moe(xn, route, w_expert_gate[0].astype(BF16), w_expert_up[0].astype(BF16),
             w_expert_down[0].astype(BF16))

    h2, q_hm, k_hm, v_hm = _proj_odd(h1, y, route, mix_norm_odd[0], w_in_odd[0].astype(BF16),
                                     batch, seq)
    branches = [_dilated_branch(q_hm, k_hm, v_hm, dil) for (_, dil) in DIL_PATTERNS]
    router = _router_params(w_router_group[1], w_router_expert[1],
                            b_router_group[1], b_router_expert[1])
    h3, xn, route = _outproj_odd(h2, branches, w_out_odd[0].astype(BF16), ffn_norm[1], router, seq)
    y = _moe(xn, route, w_expert_gate[1].astype(BF16), w_expert_up[1].astype(BF16),
             w_expert_down[1].astype(BF16))
    return _final(h3, y, route, final_norm).reshape(batch, seq, d)
```

```python
import functools

import jax
import jax.numpy as jnp
from jax import lax
from jax.experimental import pallas as pl
from jax.experimental.pallas import tpu as pltpu

F32 = jnp.float32
BF16 = jnp.bfloat16

D_MODEL = 1024
HEAD_DIM = 64
ROPE_THETA = 10000.0
RET_THETA = 10000.0
NORM_EPS = 1e-6

MOBA_HEADS = 8
MOBA_BLOCK = 256
MOBA_TOPK = 3
MOBA_CHUNK = 16
MOBA_WIDTH = MOBA_HEADS * HEAD_DIM
RET_HEADS = 4
RET_QK_DIM = 64
RET_V_DIM = 128
RET_CHUNK = 128
RET_QK_WIDTH = RET_HEADS * RET_QK_DIM
RET_V_WIDTH = RET_HEADS * RET_V_DIM
EVEN_IN_WIDTH = 3 * MOBA_WIDTH + 2 * RET_QK_WIDTH + 2 * RET_V_WIDTH
EVEN_QKV_WIDTH = EVEN_IN_WIDTH - RET_V_WIDTH

DIL_HEADS = D_MODEL // HEAD_DIM
DIL_PATTERNS = ((128, 1), (512, 4), (2048, 16))
DIL_BLOCK = 128
ODD_IN_WIDTH = 3 * D_MODEL

MOE_GROUPS = 4
MOE_EXPERTS_PER_GROUP = 8
MOE_EXPERTS = MOE_GROUPS * MOE_EXPERTS_PER_GROUP
MOE_TOPK = 2
MOE_HIDDEN = D_MODEL // 2

LANES = 128
ROUTER_WIDTH = LANES
PROJ_ROWS = 512
EXPERT_ROWS = 256
DIL_ROWS = 1024
RET_ROWS = 512
VMEM_LIMIT = 56 * 1024 * 1024
NEG_INF = float("-inf")


def _dot(a, b):
    return jnp.dot(a, b, preferred_element_type=F32)


def _dot_nt(a, b):
    return lax.dot_general(a, b, (((1,), (1,)), ((), ())), preferred_element_type=F32)


def _dot_tn(a, b):
    return lax.dot_general(a, b, (((0,), (0,)), ((), ())), preferred_element_type=F32)


def _rms(x, gain):
    ms = jnp.mean(x * x, axis=-1, keepdims=True)
    return x * lax.rsqrt(ms + NORM_EPS) * gain


def _swap_half(a, first_half):
    return jnp.where(first_half, pltpu.roll(a, LANES - 32, 1), pltpu.roll(a, 32, 1))


def _swap_pairs(a, even):
    return jnp.where(even, pltpu.roll(a, LANES - 1, 1), pltpu.roll(a, 1, 1))


def _rotary_tables(s):
    half = HEAD_DIM // 2
    inv_freq = 1.0 / (ROPE_THETA ** (jnp.arange(half, dtype=F32) * (2.0 / HEAD_DIM)))
    ang = jnp.arange(s, dtype=F32)[:, None] * inv_freq[None, :]
    cos, sin = jnp.cos(ang), jnp.sin(ang)
    return (jnp.concatenate([cos, cos, cos, cos], axis=-1),
            jnp.concatenate([-sin, sin, -sin, sin], axis=-1))


def _retnet_tables(s):
    half = RET_QK_DIM // 2
    freq = 1.0 / (RET_THETA ** jnp.linspace(0.0, 1.0, half, dtype=F32))
    ang = jnp.arange(s, dtype=F32)[:, None] * freq[None, :]
    cos, sin = jnp.cos(ang), jnp.sin(ang)
    cos2 = jnp.repeat(cos, 2, axis=-1)
    sin2 = jnp.stack([-sin, sin], axis=-1).reshape(s, RET_QK_DIM)
    return jnp.tile(cos2, (1, 2)), jnp.tile(sin2, (1, 2))


def _retention_tables():
    c = RET_CHUNK
    log_gamma = jnp.log(1.0 - 2.0 ** (-5.0 - jnp.arange(RET_HEADS, dtype=F32)))
    pos = jnp.arange(c, dtype=F32)
    rel = pos[:, None] - pos[None, :]
    decay = jnp.where(rel >= 0, jnp.exp(log_gamma[:, None, None] * jnp.maximum(rel, 0.0)), 0.0)
    tail = jnp.exp(log_gamma[:, None] * (c - 1.0 - pos))[:, :, None]
    head = jnp.exp(log_gamma[:, None] * (pos + 1.0))[:, :, None]
    gamma_c = jnp.exp(log_gamma * c)[:, None, None]
    return decay, tail, head, gamma_c


def _stack_blocks(a, lower_lanes):
    top, bot = a[:MOBA_BLOCK], a[MOBA_BLOCK:]
    head_a = jnp.where(lower_lanes, top, pltpu.roll(bot, HEAD_DIM, 1))
    head_b = jnp.where(lower_lanes, pltpu.roll(top, HEAD_DIM, 1), bot)
    return head_a.astype(BF16), head_b.astype(BF16)


def _proj_even_kernel(x_ref, g_ref, w_ref, rc_ref, rs_ref, tc_ref, ts_ref,
                      qkv_ref, rg_ref, km_ref, kst_ref, vst_ref):
    tm = x_ref.shape[0]
    xn = _rms(x_ref[...], g_ref[...]).astype(BF16)
    lane = lax.broadcasted_iota(jnp.int32, (tm, LANES), 1)
    first_half = (lane % HEAD_DIM) < (HEAD_DIM // 2)
    even = (lane % 2) == 0
    lower_lanes = lax.broadcasted_iota(jnp.int32, (MOBA_BLOCK, LANES), 1) < HEAD_DIM
    rc, rs, tc, ts = rc_ref[...], rs_ref[...], tc_ref[...], ts_ref[...]
    scale = HEAD_DIM ** -0.5
    n_chunk = 4
    for j in range(EVEN_IN_WIDTH // (n_chunk * LANES)):
        acc = _dot(xn, w_ref[:, j * n_chunk * LANES:(j + 1) * n_chunk * LANES])
        for u in range(n_chunk):
            c = j * n_chunk + u
            a = acc[:, u * LANES:(u + 1) * LANES]
            cols = slice(c * LANES, (c + 1) * LANES)
            if c < 4:
                a = (a * rc + _swap_half(a, first_half) * rs) * scale
            elif c < 8:
                a = a * rc + _swap_half(a, first_half) * rs
                km_ref[0, :, (c - 4) * LANES:(c - 3) * LANES] = jnp.mean(
                    a.reshape(tm // MOBA_BLOCK, MOBA_BLOCK, LANES), axis=1)
                kst_ref[2 * (c - 4)], kst_ref[2 * (c - 4) + 1] = _stack_blocks(a, lower_lanes)
            elif c < 12:
                vst_ref[2 * (c - 8)], vst_ref[2 * (c - 8) + 1] = _stack_blocks(a, lower_lanes)
            elif c < 14:
                a = a * tc + _swap_pairs(a, even) * ts
            elif c < 16:
                a = (a * tc + _swap_pairs(a, even) * ts) * (RET_QK_DIM ** -0.5)
            elif c < 20:
                pass
            else:
                rg_ref[:, (c - 20) * LANES:(c - 19) * LANES] = a
                continue
            qkv_ref[:, cols] = a.astype(BF16)


def _proj_even(x2, gain, w_bf, batch, seq):
    n = x2.shape[0]
    tm = PROJ_ROWS
    assert tm == 2 * MOBA_BLOCK and seq % (MOBA_CHUNK * MOBA_BLOCK) == 0
    rc, rs = _rotary_tables(seq)
    tc, ts = _retnet_tables(seq)
    pos_blocks = seq // tm
    nblk = seq // MOBA_BLOCK
    tab = pl.BlockSpec((tm, LANES), lambda i: (i % pos_blocks, 0))
    tiles_per_chunk = MOBA_CHUNK // 2
    kst_spec = pl.BlockSpec(
        (None, MOBA_HEADS, None, MOBA_BLOCK, LANES),
        lambda i: (i // pos_blocks, 0, (i % pos_blocks) // tiles_per_chunk, 0,
                   (i % pos_blocks) % tiles_per_chunk))
    vst_spec = pl.BlockSpec(
        (None, MOBA_HEADS, None, MOBA_BLOCK, LANES),
        lambda i: (i // pos_blocks, 0, (i % pos_blocks) // 2, 0, (i % pos_blocks) % 2))
    return pl.pallas_call(
        _proj_even_kernel,
        grid=(n // tm,),
        in_specs=[
            pl.BlockSpec((tm, D_MODEL), lambda i: (i, 0)),
            pl.BlockSpec((1, D_MODEL), lambda i: (0, 0)),
            pl.BlockSpec((D_MODEL, EVEN_IN_WIDTH), lambda i: (0, 0)),
            tab, tab, tab, tab,
        ],
        out_specs=[
            pl.BlockSpec((tm, EVEN_QKV_WIDTH), lambda i: (i, 0)),
            pl.BlockSpec((tm, RET_V_WIDTH), lambda i: (i, 0)),
            pl.BlockSpec((1, tm // MOBA_BLOCK, MOBA_WIDTH), lambda i: (i, 0, 0)),
            kst_spec, vst_spec,
        ],
        out_shape=[
            jax.ShapeDtypeStruct((n, EVEN_QKV_WIDTH), BF16),
            jax.ShapeDtypeStruct((n, RET_V_WIDTH), F32),
            jax.ShapeDtypeStruct((n // tm, tm // MOBA_BLOCK, MOBA_WIDTH), F32),
            jax.ShapeDtypeStruct((batch, MOBA_HEADS, nblk // MOBA_CHUNK, MOBA_BLOCK,
                                  MOBA_CHUNK * HEAD_DIM), BF16),
            jax.ShapeDtypeStruct((batch, MOBA_HEADS, nblk // 4, MOBA_BLOCK, 4 * HEAD_DIM), BF16),
        ],
        compiler_params=pltpu.CompilerParams(
            dimension_semantics=("arbitrary",), vmem_limit_bytes=VMEM_LIMIT),
        name="proj_even",
    )(x2, gain.reshape(1, D_MODEL), w_bf, rc, rs, tc, ts)


def _moba_kernel(q_ref, ko_ref, vo_ref, km_ref, kst_ref, vst_ref, o_ref):
    i = pl.program_id(2)
    nblk = km_ref.shape[0]
    bq = MOBA_BLOCK
    kw = MOBA_CHUNK * HEAD_DIM
    n_chunks = (i + MOBA_CHUNK - 1) // MOBA_CHUNK
    blk = lax.broadcasted_iota(jnp.int32, (bq, nblk), 1)
    blk_f = blk.astype(F32)
    row = lax.broadcasted_iota(jnp.int32, (bq, bq), 0)
    col = lax.broadcasted_iota(jnp.int32, (bq, bq), 1)
    lane_kw = lax.broadcasted_iota(jnp.int32, (1, kw), 1)
    lane_blk = (lane_kw // HEAD_DIM).astype(F32)
    lane_grp = (lane_kw // MOBA_BLOCK).astype(F32)
    lane_sub = lax.broadcasted_iota(jnp.int32, (1, bq), 1) // HEAD_DIM
    outs = []
    for h in range(2):
        sl = slice(h * HEAD_DIM, (h + 1) * HEAD_DIM)
        q = q_ref[:, sl]
        km = km_ref[:, sl]
        km_hi = km.astype(BF16)
        km_lo = (km - km_hi.astype(F32)).astype(BF16)
        gate = _dot_nt(q, km_hi) + _dot_nt(q, km_lo)
        g = jnp.where(blk < i, gate, NEG_INF)
        sels = []
        for _ in range(MOBA_TOPK):
            m = jnp.max(g, axis=-1, keepdims=True)
            cand = (g == m) & (m > NEG_INF)
            idx = jnp.min(jnp.where(cand, blk_f, float(nblk)), axis=-1, keepdims=True)
            g = jnp.where(blk_f == idx, NEG_INF, g)
            sels.append(jnp.where(idx < nblk, idx, -1.0))
        sel_f = jnp.concatenate(sels, axis=0)
        sel = sel_f.astype(jnp.int32)

        s_own = jnp.where(col <= row, _dot_nt(q, ko_ref[:, sl]), NEG_INF)

        q2 = jnp.concatenate([q, q], axis=1)
        qt = jnp.tile(jnp.concatenate([q2] * MOBA_TOPK, axis=0), (1, kw // LANES))
        sel_blk = sel_f.astype(BF16)

        def scores(c, acc, h=h, qt=qt, sel_blk=sel_blk):
            ids = (lane_blk + (c * MOBA_CHUNK).astype(F32)).astype(BF16)
            lhs = jnp.where(sel_blk == ids, qt, jnp.zeros_like(qt))
            return acc + _dot_nt(lhs, kst_ref[h, c])

        s_sel = lax.fori_loop(0, n_chunks, scores, jnp.zeros((MOBA_TOPK * bq, bq), F32))
        s_sel = jnp.where(sel >= 0, s_sel, NEG_INF)
        m = jnp.max(s_own, axis=-1, keepdims=True)
        for r in range(MOBA_TOPK):
            m = jnp.maximum(m, jnp.max(s_sel[r * bq:(r + 1) * bq], axis=-1, keepdims=True))
        p_own = jnp.exp(s_own - m)
        p_sel = jnp.exp(s_sel - jnp.concatenate([m] * MOBA_TOPK, axis=0))
        den = jnp.sum(p_own, axis=-1, keepdims=True)
        for r in range(MOBA_TOPK):
            den = den + jnp.sum(p_sel[r * bq:(r + 1) * bq], axis=-1, keepdims=True)

        pt = jnp.tile(p_sel.astype(BF16), (1, kw // bq))
        sel_grp = lax.shift_right_arithmetic(sel, 2).astype(F32).astype(BF16)

        def values(c, acc, h=h, pt=pt, sel_grp=sel_grp):
            ids = (lane_grp + (c * (MOBA_CHUNK // 4)).astype(F32)).astype(BF16)
            lhs = jnp.where(sel_grp == ids, pt, jnp.zeros_like(pt))
            return acc + _dot(lhs, vst_ref[h, c])

        o4 = lax.fori_loop(0, n_chunks, values, jnp.zeros((MOBA_TOPK * bq, bq), F32))
        o4 = jnp.where(lane_sub == (sel & 3), o4, 0.0)
        x = o4[:bq]
        for r in range(1, MOBA_TOPK):
            x = x + o4[r * bq:(r + 1) * bq]
        y = x[:, :LANES] + x[:, LANES:]
        y = y + pltpu.roll(y, HEAD_DIM, 1)
        outs.append((_dot(p_own.astype(BF16), vo_ref[:, sl]) + y[:, :HEAD_DIM]) / den)
    o_ref[...] = jnp.concatenate(outs, axis=-1).astype(BF16)


def _moba(qkv3, kmean3, kst, vst):
    b, s, _ = qkv3.shape
    nblk = s // MOBA_BLOCK
    pairs = MOBA_HEADS // 2
    n_chunks = nblk // MOBA_CHUNK
    kw = MOBA_CHUNK * HEAD_DIM
    own = lambda t: pl.BlockSpec((None, MOBA_BLOCK, LANES), lambda bi, hp, i: (bi, i, t * pairs + hp))
    return pl.pallas_call(
        _moba_kernel,
        grid=(b, pairs, nblk),
        in_specs=[
            own(0), own(1), own(2),
            pl.BlockSpec((None, nblk, LANES), lambda bi, hp, i: (bi, 0, hp)),
            pl.BlockSpec((None, 2, n_chunks, MOBA_BLOCK, kw), lambda bi, hp, i: (bi, hp, 0, 0, 0)),
            pl.BlockSpec((None, 2, n_chunks, kw, MOBA_BLOCK), lambda bi, hp, i: (bi, hp, 0, 0, 0)),
        ],
        out_specs=own(0),
        out_shape=jax.ShapeDtypeStruct((b, s, MOBA_WIDTH), BF16),
        compiler_params=pltpu.CompilerParams(
            dimension_semantics=("arbitrary", "arbitrary", "arbitrary"),
            vmem_limit_bytes=VMEM_LIMIT),
        name="moba",
    )(qkv3, qkv3, qkv3, kmean3, kst, vst.reshape(b, MOBA_HEADS, n_chunks, kw, MOBA_BLOCK))


def _retention_kernel(q_ref, k_ref, v_ref, g_ref, dec_ref, tail_ref, head_ref, gc_ref,
                      o_ref, state_ref):
    @pl.when(pl.program_id(2) == 0)
    def _():
        state_ref[...] = jnp.zeros_like(state_ref)

    c = RET_CHUNK
    for h in range(2):
        qk = slice(h * RET_QK_DIM, (h + 1) * RET_QK_DIM)
        vv = slice(h * RET_V_DIM, (h + 1) * RET_V_DIM)
        dec, tail, head, gc = dec_ref[h], tail_ref[h], head_ref[h], gc_ref[h]
        for u in range(q_ref.shape[0] // c):
            rows = slice(u * c, (u + 1) * c)
            q, k, v = q_ref[rows, qk], k_ref[rows, qk], v_ref[rows, vv]
            scores = _dot_nt(q, k) * dec
            state = state_ref[h]
            y = (_dot(scores.astype(BF16), v)
                 + _dot((q.astype(F32) * head).astype(BF16), state.astype(BF16)))
            state_ref[h] = gc * state + _dot_tn((k.astype(F32) * tail).astype(BF16), v)
            mu = jnp.mean(y, axis=-1, keepdims=True)
            yc = y - mu
            var = jnp.mean(yc * yc, axis=-1, keepdims=True)
            gate = g_ref[rows, vv]
            swish = gate / (1.0 + jnp.exp(-gate))
            o_ref[rows, vv] = (yc * lax.rsqrt(var + NORM_EPS) * swish).astype(BF16)


def _retention(qkv3, rg3):
    b, s, _ = qkv3.shape
    t = RET_ROWS
    pairs = RET_HEADS // 2
    dec, tail, head, gc = _retention_tables()
    q0 = 3 * MOBA_WIDTH // LANES
    k0 = q0 + RET_QK_WIDTH // LANES
    v0 = (3 * MOBA_WIDTH + 2 * RET_QK_WIDTH) // (2 * RET_V_DIM)
    return pl.pallas_call(
        _retention_kernel,
        grid=(b, pairs, s // t),
        in_specs=[
            pl.BlockSpec((None, t, LANES), lambda bi, hp, n: (bi, n, q0 + hp)),
            pl.BlockSpec((None, t, LANES), lambda bi, hp, n: (bi, n, k0 + hp)),
            pl.BlockSpec((None, t, 2 * RET_V_DIM), lambda bi, hp, n: (bi, n, v0 + hp)),
            pl.BlockSpec((None, t, 2 * RET_V_DIM), lambda bi, hp, n: (bi, n, hp)),
            pl.BlockSpec((2, RET_CHUNK, RET_CHUNK), lambda bi, hp, n: (hp, 0, 0)),
            pl.BlockSpec((2, RET_CHUNK, 1), lambda bi, hp, n: (hp, 0, 0)),
            pl.BlockSpec((2, RET_CHUNK, 1), lambda bi, hp, n: (hp, 0, 0)),
            pl.BlockSpec((2, 1, 1), lambda bi, hp, n: (hp, 0, 0)),
        ],
        out_specs=pl.BlockSpec((None, t, 2 * RET_V_DIM), lambda bi, hp, n: (bi, n, hp)),
        out_shape=jax.ShapeDtypeStruct((b, s, RET_V_WIDTH), BF16),
        scratch_shapes=[pltpu.VMEM((2, RET_QK_DIM, RET_V_DIM), F32)],
        compiler_params=pltpu.CompilerParams(
            dimension_semantics=("arbitrary", "arbitrary", "arbitrary"),
            vmem_limit_bytes=VMEM_LIMIT),
        name="retention",
    )(qkv3, qkv3, qkv3, rg3, dec, tail, head, gc)


def _route(xn, wr_hi_ref, wr_lo_ref, br_ref):
    x_hi = xn.astype(BF16)
    x_lo = (xn - x_hi.astype(F32)).astype(BF16)
    w_hi = wr_hi_ref[...]
    z = _dot(x_hi, w_hi) + _dot(x_lo, w_hi) + _dot(x_hi, wr_lo_ref[...]) + br_ref[...]
    lane = lax.broadcasted_iota(jnp.int32, z.shape, 1)

    def top1(v):
        m = jnp.max(v, axis=-1, keepdims=True)
        return m, jnp.min(jnp.where(v == m, lane, ROUTER_WIDTH), axis=-1, keepdims=True)

    g_logit = jnp.where(lane < MOE_GROUPS, z, NEG_INF)
    g_max, grp = top1(g_logit)
    g_val = 1.0 / jnp.sum(jnp.exp(g_logit - g_max), axis=-1, keepdims=True)
    lo = MOE_GROUPS + MOE_EXPERTS_PER_GROUP * grp
    f = jnp.where((lane >= lo) & (lane < lo + MOE_EXPERTS_PER_GROUP), z, NEG_INF)
    v1, i1 = top1(f)
    v2, i2 = top1(jnp.where(lane == i1, NEG_INF, f))
    t = jnp.exp(v2 - v1)
    g1 = g_val / (1.0 + t)
    g2 = g_val * t / (1.0 + t)
    e1 = (i1 - MOE_GROUPS).astype(F32)
    e2 = (i2 - MOE_GROUPS).astype(F32)
    return jnp.where(lane == 0, e1, jnp.where(lane == 1, e2, jnp.where(
        lane == 2, g1, jnp.where(lane == 3, g2, 0.0))))


def _outproj_even_kernel(x_ref, a_ref, r_ref, wo_ref, gn_ref, wr_hi_ref, wr_lo_ref, br_ref,
                         h_ref, xn_ref, rt_ref):
    h = x_ref[...] + (_dot(a_ref[...], wo_ref[:MOBA_WIDTH, :])
                      + _dot(r_ref[...], wo_ref[MOBA_WIDTH:, :]))
    h_ref[...] = h
    xn = _rms(h, gn_ref[...])
    xn_ref[...] = xn
    rt_ref[...] = _route(xn, wr_hi_ref, wr_lo_ref, br_ref)


def _outproj_even(x2, a2, r2, wo_bf, gain, router):
    n = x2.shape[0]
    tm = PROJ_ROWS
    row = lambda w: pl.BlockSpec((tm, w), lambda i: (i, 0))
    full = lambda r, w: pl.BlockSpec((r, w), lambda i: (0, 0))
    return pl.pallas_call(
        _outproj_even_kernel,
        grid=(n // tm,),
        in_specs=[row(D_MODEL), row(MOBA_WIDTH), row(RET_V_WIDTH), full(D_MODEL, D_MODEL),
                  full(1, D_MODEL), full(D_MODEL, ROUTER_WIDTH), full(D_MODEL, ROUTER_WIDTH),
                  full(1, ROUTER_WIDTH)],
        out_specs=[row(D_MODEL), row(D_MODEL), row(ROUTER_WIDTH)],
        out_shape=[jax.ShapeDtypeStruct((n, D_MODEL), F32),
                   jax.ShapeDtypeStruct((n, D_MODEL), F32),
                   jax.ShapeDtypeStruct((n, ROUTER_WIDTH), F32)],
        compiler_params=pltpu.CompilerParams(
            dimension_semantics=("arbitrary",), vmem_limit_bytes=VMEM_LIMIT),
        name="outproj_even",
    )(x2, a2, r2, wo_bf, gain.reshape(1, D_MODEL), *router)


def _outproj_odd_kernel(h_ref, o1_ref, o2_ref, o3_ref, l1_ref, l2_ref, l3_ref, wo_ref,
                        gn_ref, wr_hi_ref, wr_lo_ref, br_ref, hout_ref, xn_ref, rt_ref):
    pieces = []
    for hp in range(DIL_HEADS // 2):
        la, lb, lc = l1_ref[hp], l2_ref[hp], l3_ref[hp]
        m = jnp.maximum(jnp.maximum(la, lb), lc)
        ea, eb, ec = jnp.exp(la - m), jnp.exp(lb - m), jnp.exp(lc - m)
        mix = (ea * o1_ref[hp] + eb * o2_ref[hp] + ec * o3_ref[hp]) / (ea + eb + ec)
        pieces.append(mix.astype(BF16))
    h = h_ref[...] + _dot(jnp.concatenate(pieces, axis=-1), wo_ref[...])
    hout_ref[...] = h
    xn = _rms(h, gn_ref[...])
    xn_ref[...] = xn
    rt_ref[...] = _route(xn, wr_hi_ref, wr_lo_ref, br_ref)


def _outproj_odd(h2, branches, wo_bf, gain, router, seq):
    n = h2.shape[0]
    tm = PROJ_ROWS
    pairs = DIL_HEADS // 2
    per_seq = seq // tm
    row = lambda w: pl.BlockSpec((tm, w), lambda i: (i, 0))
    full = lambda r, w: pl.BlockSpec((r, w), lambda i: (0, 0))
    hm = pl.BlockSpec((None, pairs, tm, LANES), lambda i: (i // per_seq, 0, i % per_seq, 0))
    outs = [o for (o, _) in branches]
    lses = [l for (_, l) in branches]
    return pl.pallas_call(
        _outproj_odd_kernel,
        grid=(n // tm,),
        in_specs=[row(D_MODEL), hm, hm, hm, hm, hm, hm, full(D_MODEL, D_MODEL),
                  full(1, D_MODEL), full(D_MODEL, ROUTER_WIDTH), full(D_MODEL, ROUTER_WIDTH),
                  full(1, ROUTER_WIDTH)],
        out_specs=[row(D_MODEL), row(D_MODEL), row(ROUTER_WIDTH)],
        out_shape=[jax.ShapeDtypeStruct((n, D_MODEL), F32),
                   jax.ShapeDtypeStruct((n, D_MODEL), F32),
                   jax.ShapeDtypeStruct((n, ROUTER_WIDTH), F32)],
        compiler_params=pltpu.CompilerParams(
            dimension_semantics=("arbitrary",), vmem_limit_bytes=VMEM_LIMIT),
        name="outproj_odd",
    )(h2, *outs, *lses, wo_bf, gain.reshape(1, D_MODEL), *router)


def _plan_rows(expert):
    rb = EXPERT_ROWS
    n_assign = expert.size
    e_flat = expert.T.reshape(-1)
    order = jnp.argsort(e_flat).astype(jnp.int32)
    e_s = e_flat[order]
    counts = jnp.bincount(e_flat, length=MOE_EXPERTS).astype(jnp.int32)
    starts = jnp.cumsum(counts) - counts
    padded = ((counts + rb - 1) // rb) * rb
    p_ends = jnp.cumsum(padded)
    p_starts = p_ends - padded
    dest = p_starts[e_s] + (jnp.arange(n_assign, dtype=jnp.int32) - starts[e_s])
    n_blocks = n_assign // rb + MOE_EXPERTS
    n_rows = n_blocks * rb
    row_dst = jnp.zeros((n_rows,), jnp.int32).at[dest].set(order)
    blk_start = jnp.arange(n_blocks, dtype=jnp.int32) * rb
    blk_exp = jnp.minimum(jnp.sum(p_ends[None, :] <= blk_start[:, None], axis=1),
                          MOE_EXPERTS - 1).astype(jnp.int32)
    blk_valid = jnp.clip(counts[blk_exp] - (blk_start - p_starts[blk_exp]), 0, rb).astype(jnp.int32)
    n_used = (p_ends[-1] // rb).astype(jnp.int32).reshape(1)
    return blk_exp, blk_valid, row_dst, n_used


def _expert_kernel(blk_exp, blk_valid, row_dst, n_used,
                   wg_ref, wu_ref, wd_ref, x_hbm, out_hbm,
                   xbuf, ybuf, gsem, ssem):
    del blk_exp
    rb = EXPERT_ROWS
    n_tok = x_hbm.shape[0]
    i = pl.program_id(0)
    used = n_used[0]
    slot = i % 2

    def gather_copy(tok, r, s):
        return pltpu.make_async_copy(x_hbm.at[pl.ds(tok, 1)], xbuf.at[s, pl.ds(r, 1)], gsem.at[s])

    def scatter_copy(dst, r, s):
        return pltpu.make_async_copy(ybuf.at[s, pl.ds(r, 1)], out_hbm.at[pl.ds(dst, 1)], ssem.at[s])

    def start_gather(b, s):
        def body(r, carry):
            a = row_dst[b * rb + r]
            gather_copy(a - jnp.where(a >= n_tok, n_tok, 0), r, s).start()
            return carry
        lax.fori_loop(0, rb, body, 0, unroll=8)

    def wait_scatter(count, s):
        wide = 16

        def wide_body(r, carry):
            pltpu.make_async_copy(ybuf.at[s, pl.ds(0, wide)], out_hbm.at[pl.ds(0, wide)],
                                  ssem.at[s]).wait()
            return carry
        lax.fori_loop(0, count // wide, wide_body, 0)

        def body(r, carry):
            scatter_copy(0, 0, s).wait()
            return carry
        lax.fori_loop(0, count % wide, body, 0)

    @pl.when(i == 0)
    def _():
        start_gather(0, 0)

    @pl.when(i < used)
    def _():
        @pl.when(i + 1 < used)
        def _():
            start_gather(i + 1, 1 - slot)

        pltpu.make_async_copy(x_hbm.at[pl.ds(0, rb)], xbuf.at[slot], gsem.at[slot]).wait()

        @pl.when(i >= 2)
        def _():
            wait_scatter(blk_valid[i - 2], slot)

        xb = xbuf[slot].astype(BF16)
        g = _dot(xb, wg_ref[...])
        u = _dot(xb, wu_ref[...])
        hid = (g / (1.0 + jnp.exp(-g)) * u).astype(BF16)
        ybuf[slot] = _dot(hid, wd_ref[...])

        def scatter_body(r, carry):
            scatter_copy(row_dst[i * rb + r], r, slot).start()
            return carry
        def scatter_body8(r8, carry):
            for j in range(8):
                scatter_body(r8 * 8 + j, carry)
            return carry
        n_valid = blk_valid[i]
        lax.fori_loop(0, n_valid // 8, scatter_body8, 0)
        lax.fori_loop((n_valid // 8) * 8, n_valid, scatter_body, 0)

        @pl.when(i == used - 1)
        def _():
            wait_scatter(blk_valid[i], slot)

            @pl.when(i >= 1)
            def _():
                wait_scatter(blk_valid[i - 1], 1 - slot)


def _experts(xn, plan, wg_bf, wu_bf, wd_bf):
    blk_exp, blk_valid, row_dst, n_used = plan
    n = xn.shape[0]
    rb = EXPERT_ROWS
    n_blocks = blk_exp.shape[0]
    w_in = pl.BlockSpec((None, D_MODEL, MOE_HIDDEN), lambda i, be, bv, rd, nu: (be[i], 0, 0))
    w_out = pl.BlockSpec((None, MOE_HIDDEN, D_MODEL), lambda i, be, bv, rd, nu: (be[i], 0, 0))
    return pl.pallas_call(
        _expert_kernel,
        grid_spec=pltpu.PrefetchScalarGridSpec(
            num_scalar_prefetch=4,
            grid=(n_blocks,),
            in_specs=[
                w_in, w_in, w_out,
                pl.BlockSpec(memory_space=pl.ANY),
            ],
            out_specs=pl.BlockSpec(memory_space=pl.ANY),
            scratch_shapes=[
                pltpu.VMEM((2, rb, D_MODEL), F32),
                pltpu.VMEM((2, rb, D_MODEL), F32),
                pltpu.SemaphoreType.DMA((2,)),
                pltpu.SemaphoreType.DMA((2,)),
            ],
        ),
        out_shape=jax.ShapeDtypeStruct((MOE_TOPK * n, D_MODEL), F32),
        compiler_params=pltpu.CompilerParams(
            dimension_semantics=("arbitrary",), vmem_limit_bytes=VMEM_LIMIT),
        name="experts",
    )(blk_exp, blk_valid, row_dst, n_used, wg_bf, wu_bf, wd_bf, xn)


def _moe(xn, route, wg_bf, wu_bf, wd_bf):
    expert = route[:, :MOE_TOPK].astype(jnp.int32)
    return _experts(xn, _plan_rows(expert), wg_bf, wu_bf, wd_bf)


def _moe_combine(h_ref, y0_ref, y1_ref, rt_ref):
    rt = rt_ref[...]
    return h_ref[...] + (y0_ref[...] * rt[:, 2:3] + y1_ref[...] * rt[:, 3:4])


def _moe_output_specs(n, tm):
    return [pl.BlockSpec((tm, D_MODEL), lambda i: (i, 0)),
            pl.BlockSpec((tm, D_MODEL), lambda i: (n // tm + i, 0))]


def _proj_odd_kernel(h_ref, y0_ref, y1_ref, rt_ref, g_ref, w_ref, rc_ref, rs_ref,
                     hout_ref, q_ref, k_ref, v_ref):
    tm = h_ref.shape[0]
    h = _moe_combine(h_ref, y0_ref, y1_ref, rt_ref)
    hout_ref[...] = h
    xn = _rms(h, g_ref[...]).astype(BF16)
    lane = lax.broadcasted_iota(jnp.int32, (tm, LANES), 1)
    first_half = (lane % HEAD_DIM) < (HEAD_DIM // 2)
    rc, rs = rc_ref[...], rs_ref[...]
    scale = HEAD_DIM ** -0.5
    n_chunk = 4
    pairs = DIL_HEADS // 2
    for j in range(ODD_IN_WIDTH // (n_chunk * LANES)):
        acc = _dot(xn, w_ref[:, j * n_chunk * LANES:(j + 1) * n_chunk * LANES])
        for u in range(n_chunk):
            c = j * n_chunk + u
            a = acc[:, u * LANES:(u + 1) * LANES]
            if c < pairs:
                q_ref[c] = ((a * rc + _swap_half(a, first_half) * rs) * scale).astype(BF16)
            elif c < 2 * pairs:
                k_ref[c - pairs] = (a * rc + _swap_half(a, first_half) * rs).astype(BF16)
            else:
                v_ref[c - 2 * pairs] = a.astype(BF16)


def _proj_odd(h1, y2, route, gain, w_bf, batch, seq):
    n = h1.shape[0]
    tm = PROJ_ROWS
    pairs = DIL_HEADS // 2
    rc, rs = _rotary_tables(seq)
    per_seq = seq // tm
    tab = pl.BlockSpec((tm, LANES), lambda i: (i % per_seq, 0))
    hm = pl.BlockSpec((None, pairs, tm, LANES), lambda i: (i // per_seq, 0, i % per_seq, 0))
    hm_shape = jax.ShapeDtypeStruct((batch, pairs, seq, LANES), BF16)
    return pl.pallas_call(
        _proj_odd_kernel,
        grid=(n // tm,),
        in_specs=[
            pl.BlockSpec((tm, D_MODEL), lambda i: (i, 0)),
            *_moe_output_specs(n, tm),
            pl.BlockSpec((tm, ROUTER_WIDTH), lambda i: (i, 0)),
            pl.BlockSpec((1, D_MODEL), lambda i: (0, 0)),
            pl.BlockSpec((D_MODEL, ODD_IN_WIDTH), lambda i: (0, 0)),
            tab, tab,
        ],
        out_specs=[pl.BlockSpec((tm, D_MODEL), lambda i: (i, 0)), hm, hm, hm],
        out_shape=[jax.ShapeDtypeStruct((n, D_MODEL), F32), hm_shape, hm_shape, hm_shape],
        compiler_params=pltpu.CompilerParams(
            dimension_semantics=("arbitrary",), vmem_limit_bytes=VMEM_LIMIT),
        name="proj_odd",
    )(h1, y2, y2, route, gain.reshape(1, D_MODEL), w_bf, rc, rs)


def _dilated_kernel(q_ref, kp_ref, ko_ref, vp_ref, vo_ref, o_ref, l_ref):
    n = pl.program_id(3)
    blk = DIL_BLOCK
    a = lax.broadcasted_iota(jnp.int32, (blk, 2 * blk), 0)
    e = lax.broadcasted_iota(jnp.int32, (blk, 2 * blk), 1)
    dist = blk + a - e
    band = (dist >= 0) & (dist <= blk)
    first = band & ((n > 0) | (e >= blk))
    for u in range(q_ref.shape[0] // blk):
        rows = slice(u * blk, (u + 1) * blk)
        o_parts, l_parts = [], []
        for h in range(2):
            sl = slice(h * HEAD_DIM, (h + 1) * HEAD_DIM)
            q = q_ref[rows, sl]
            if u == 0:
                k = jnp.concatenate([kp_ref[:, sl], ko_ref[:blk, sl]], axis=0)
                v = jnp.concatenate([vp_ref[:, sl], vo_ref[:blk, sl]], axis=0)
                mask = first
            else:
                k = ko_ref[(u - 1) * blk:(u + 1) * blk, sl]
                v = vo_ref[(u - 1) * blk:(u + 1) * blk, sl]
                mask = band
            s = jnp.where(mask, _dot_nt(q, k), NEG_INF)
            m = jnp.max(s, axis=-1, keepdims=True)
            p = jnp.exp(s - m)
            den = jnp.sum(p, axis=-1, keepdims=True)
            o_parts.append(_dot((p / den).astype(BF16), v))
            l_parts.append(jnp.broadcast_to(m + jnp.log(den), (blk, HEAD_DIM)))
        o_ref[rows, :] = jnp.concatenate(o_parts, axis=-1).astype(o_ref.dtype)
        l_ref[rows, :] = jnp.concatenate(l_parts, axis=-1)


def _dilated_branch(q_hm, k_hm, v_hm, dil):
    b, pairs, s, _ = q_hm.shape
    length = s // dil
    tq = min(DIL_ROWS, length)
    sub = tq // DIL_BLOCK
    view = lambda t: t.reshape(b, pairs, length, dil * LANES)
    own = pl.BlockSpec((None, None, tq, LANES), lambda bi, hp, r, n: (bi, hp, n, r))
    prev = pl.BlockSpec((None, None, DIL_BLOCK, LANES),
                        lambda bi, hp, r, n: (bi, hp, jnp.maximum(n * sub - 1, 0), r))
    shape = lambda dt: jax.ShapeDtypeStruct((b, pairs, length, dil * LANES), dt)
    o, lse = pl.pallas_call(
        _dilated_kernel,
        grid=(b, pairs, dil, length // tq),
        in_specs=[own, prev, own, prev, own],
        out_specs=[own, own],
        out_shape=[shape(BF16), shape(F32)],
        compiler_params=pltpu.CompilerParams(
            dimension_semantics=("arbitrary",) * 4, vmem_limit_bytes=VMEM_LIMIT),
        name=f"dilated_{dil}",
    )(view(q_hm), view(k_hm), view(k_hm), view(v_hm), view(v_hm))
    return o.reshape(b, pairs, s, LANES), lse.reshape(b, pairs, s, LANES)


def _final_kernel(h_ref, y0_ref, y1_ref, rt_ref, g_ref, o_ref):
    o_ref[...] = _rms(_moe_combine(h_ref, y0_ref, y1_ref, rt_ref), g_ref[...])


def _final(h, y2, route, gain):
    n = h.shape[0]
    tm = PROJ_ROWS
    return pl.pallas_call(
        _final_kernel,
        grid=(n // tm,),
        in_specs=[pl.BlockSpec((tm, D_MODEL), lambda i: (i, 0)),
                  *_moe_output_specs(n, tm),
                  pl.BlockSpec((tm, ROUTER_WIDTH), lambda i: (i, 0)),
                  pl.BlockSpec((1, D_MODEL), lambda i: (0, 0))],
        out_specs=pl.BlockSpec((tm, D_MODEL), lambda i: (i, 0)),
        out_shape=jax.ShapeDtypeStruct((n, D_MODEL), F32),
        compiler_params=pltpu.CompilerParams(
            dimension_semantics=("arbitrary",), vmem_limit_bytes=VMEM_LIMIT),
        name="final_norm",
    )(h, y2, y2, route, gain.reshape(1, D_MODEL))


def _router_params(w_group, w_fine, b_group, b_fine):
    w = jnp.concatenate([w_group, w_fine], axis=-1).astype(F32)
    pad = ROUTER_WIDTH - w.shape[1]
    w = jnp.pad(w, ((0, 0), (0, pad)))
    bias = jnp.pad(jnp.concatenate([b_group, b_fine]).astype(F32), (0, pad)).reshape(1, ROUTER_WIDTH)
    w_hi = w.astype(BF16)
    return w_hi, (w - w_hi.astype(F32)).astype(BF16), bias


def kernel(x, mix_norm_even, w_in_even, w_out_even, mix_norm_odd, w_in_odd, w_out_odd, ffn_norm,
           w_router_group, b_router_group, w_router_expert, b_router_expert,
           w_expert_gate, w_expert_up, w_expert_down, final_norm):
    batch, seq, d = x.shape
    n = batch * seq
    x2 = x.reshape(n, d)

    qkv, rg, kmean, kst, vst = _proj_even(x2, mix_norm_even[0], w_in_even[0].astype(BF16),
                                          batch, seq)
    qkv3 = qkv.reshape(batch, seq, EVEN_QKV_WIDTH)
    a_out = _moba(qkv3, kmean.reshape(batch, seq // MOBA_BLOCK, MOBA_WIDTH), kst, vst)
    r_out = _retention(qkv3, rg.reshape(batch, seq, RET_V_WIDTH))
    router = _router_params(w_router_group[0], w_router_expert[0],
                            b_router_group[0], b_router_expert[0])
    h1, xn, route = _outproj_even(x2, a_out.reshape(n, MOBA_WIDTH), r_out.reshape(n, RET_V_WIDTH),
                                  w_out_even[0].astype(BF16), ffn_norm[0], router)
    y = _moe(xn, route, w_expert_gate[0].astype(BF16), w_expert_up[0].astype(BF16),
             w_expert_down[0].astype(BF16))

    h2, q_hm, k_hm, v_hm = _proj_odd(h1, y, route, mix_norm_odd[0], w_in_odd[0].astype(BF16),
                                     batch, seq)
    branches = [_dilated_branch(q_hm, k_hm, v_hm, dil) for (_, dil) in DIL_PATTERNS]
    router = _router_params(w_router_group[1], w_router_expert[1],
                            b_router_group[1], b_router_expert[1])
    h3, xn, route = _outproj_odd(h2, branches, w_out_odd[0].astype(BF16), ffn_norm[1], router, seq)
    y = _moe(xn, route, w_expert_gate[1].astype(BF16), w_expert_up[1].astype(BF16),
             w_expert_down[1].astype(BF16))
    return _final(h3, y, route, final_norm).reshape(batch, seq, d)
```

```python
import functools

import jax
import jax.numpy as jnp
from jax import lax
from jax.experimental import pallas as pl
from jax.experimental.pallas import tpu as pltpu

F32 = jnp.float32
BF16 = jnp.bfloat16

D_MODEL = 1024
HEAD_DIM = 64
ROPE_THETA = 10000.0
RET_THETA = 10000.0
NORM_EPS = 1e-6

MOBA_HEADS = 8
MOBA_BLOCK = 256
MOBA_TOPK = 3
MOBA_CHUNK = 16
MOBA_WIDTH = MOBA_HEADS * HEAD_DIM
RET_HEADS = 4
RET_QK_DIM = 64
RET_V_DIM = 128
RET_CHUNK = 128
RET_QK_WIDTH = RET_HEADS * RET_QK_DIM
RET_V_WIDTH = RET_HEADS * RET_V_DIM
EVEN_IN_WIDTH = 3 * MOBA_WIDTH + 2 * RET_QK_WIDTH + 2 * RET_V_WIDTH
EVEN_QKV_WIDTH = EVEN_IN_WIDTH - RET_V_WIDTH

DIL_HEADS = D_MODEL // HEAD_DIM
DIL_PATTERNS = ((128, 1), (512, 4), (2048, 16))
DIL_BLOCK = 128
ODD_IN_WIDTH = 3 * D_MODEL

MOE_GROUPS = 4
MOE_EXPERTS_PER_GROUP = 8
MOE_EXPERTS = MOE_GROUPS * MOE_EXPERTS_PER_GROUP
MOE_TOPK = 2
MOE_HIDDEN = D_MODEL // 2

LANES = 128
ROUTER_WIDTH = LANES
PROJ_ROWS = 512
EXPERT_ROWS = 256
DIL_ROWS = 1024
RET_ROWS = 512
VMEM_LIMIT = 56 * 1024 * 1024
NEG_INF = float("-inf")


def _dot(a, b):
    return jnp.dot(a, b, preferred_element_type=F32)


def _dot_nt(a, b):
    return lax.dot_general(a, b, (((1,), (1,)), ((), ())), preferred_element_type=F32)


def _dot_tn(a, b):
    return lax.dot_general(a, b, (((0,), (0,)), ((), ())), preferred_element_type=F32)


def _rms(x, gain):
    ms = jnp.mean(x * x, axis=-1, keepdims=True)
    return x * lax.rsqrt(ms + NORM_EPS) * gain


def _swap_half(a, first_half):
    return jnp.where(first_half, pltpu.roll(a, LANES - 32, 1), pltpu.roll(a, 32, 1))


def _swap_pairs(a, even):
    return jnp.where(even, pltpu.roll(a, LANES - 1, 1), pltpu.roll(a, 1, 1))


def _rotary_tables(s):
    half = HEAD_DIM // 2
    inv_freq = 1.0 / (ROPE_THETA ** (jnp.arange(half, dtype=F32) * (2.0 / HEAD_DIM)))
    ang = jnp.arange(s, dtype=F32)[:, None] * inv_freq[None, :]
    cos, sin = jnp.cos(ang), jnp.sin(ang)
    return (jnp.concatenate([cos, cos, cos, cos], axis=-1),
            jnp.concatenate([-sin, sin, -sin, sin], axis=-1))


def _retnet_tables(s):
    half = RET_QK_DIM // 2
    freq = 1.0 / (RET_THETA ** jnp.linspace(0.0, 1.0, half, dtype=F32))
    ang = jnp.arange(s, dtype=F32)[:, None] * freq[None, :]
    cos, sin = jnp.cos(ang), jnp.sin(ang)
    cos2 = jnp.repeat(cos, 2, axis=-1)
    sin2 = jnp.stack([-sin, sin], axis=-1).reshape(s, RET_QK_DIM)
    return jnp.tile(cos2, (1, 2)), jnp.tile(sin2, (1, 2))


def _retention_tables():
    c = RET_CHUNK
    log_gamma = jnp.log(1.0 - 2.0 ** (-5.0 - jnp.arange(RET_HEADS, dtype=F32)))
    pos = jnp.arange(c, dtype=F32)
    rel = pos[:, None] - pos[None, :]
    decay = jnp.where(rel >= 0, jnp.exp(log_gamma[:, None, None] * jnp.maximum(rel, 0.0)), 0.0)
    tail = jnp.exp(log_gamma[:, None] * (c - 1.0 - pos))[:, :, None]
    head = jnp.exp(log_gamma[:, None] * (pos + 1.0))[:, :, None]
    gamma_c = jnp.exp(log_gamma * c)[:, None, None]
    return decay, tail, head, gamma_c


def _stack_blocks(a, lower_lanes):
    top, bot = a[:MOBA_BLOCK], a[MOBA_BLOCK:]
    head_a = jnp.where(lower_lanes, top, pltpu.roll(bot, HEAD_DIM, 1))
    head_b = jnp.where(lower_lanes, pltpu.roll(top, HEAD_DIM, 1), bot)
    return head_a.astype(BF16), head_b.astype(BF16)


def _proj_even_kernel(x_ref, g_ref, w_ref, rc_ref, rs_ref, tc_ref, ts_ref,
                      qkv_ref, rg_ref, km_ref, kst_ref, vst_ref):
    tm = x_ref.shape[0]
    xn = _rms(x_ref[...], g_ref[...]).astype(BF16)
    lane = lax.broadcasted_iota(jnp.int32, (tm, LANES), 1)
    first_half = (lane % HEAD_DIM) < (HEAD_DIM // 2)
    even = (lane % 2) == 0
    lower_lanes = lax.broadcasted_iota(jnp.int32, (MOBA_BLOCK, LANES), 1) < HEAD_DIM
    rc, rs, tc, ts = rc_ref[...], rs_ref[...], tc_ref[...], ts_ref[...]
    scale = HEAD_DIM ** -0.5
    n_chunk = 4
    for j in range(EVEN_IN_WIDTH // (n_chunk * LANES)):
        acc = _dot(xn, w_ref[:, j * n_chunk * LANES:(j + 1) * n_chunk * LANES])
        for u in range(n_chunk):
            c = j * n_chunk + u
            a = acc[:, u * LANES:(u + 1) * LANES]
            cols = slice(c * LANES, (c + 1) * LANES)
            if c < 4:
                a = (a * rc + _swap_half(a, first_half) * rs) * scale
            elif c < 8:
                a = a * rc + _swap_half(a, first_half) * rs
                km_ref[0, :, (c - 4) * LANES:(c - 3) * LANES] = jnp.mean(
                    a.reshape(tm // MOBA_BLOCK, MOBA_BLOCK, LANES), axis=1)
                kst_ref[2 * (c - 4)], kst_ref[2 * (c - 4) + 1] = _stack_blocks(a, lower_lanes)
            elif c < 12:
                vst_ref[2 * (c - 8)], vst_ref[2 * (c - 8) + 1] = _stack_blocks(a, lower_lanes)
            elif c < 14:
                a = a * tc + _swap_pairs(a, even) * ts
            elif c < 16:
                a = (a * tc + _swap_pairs(a, even) * ts) * (RET_QK_DIM ** -0.5)
            elif c < 20:
                pass
            else:
                rg_ref[:, (c - 20) * LANES:(c - 19) * LANES] = a
                continue
            qkv_ref[:, cols] = a.astype(BF16)


def _proj_even(x2, gain, w_bf, batch, seq):
    n = x2.shape[0]
    tm = PROJ_ROWS
    assert tm == 2 * MOBA_BLOCK and seq % (MOBA_CHUNK * MOBA_BLOCK) == 0
    rc, rs = _rotary_tables(seq)
    tc, ts = _retnet_tables(seq)
    pos_blocks = seq // tm
    nblk = seq // MOBA_BLOCK
    tab = pl.BlockSpec((tm, LANES), lambda i: (i % pos_blocks, 0))
    tiles_per_chunk = MOBA_CHUNK // 2
    kst_spec = pl.BlockSpec(
        (None, MOBA_HEADS, None, MOBA_BLOCK, LANES),
        lambda i: (i // pos_blocks, 0, (i % pos_blocks) // tiles_per_chunk, 0,
                   (i % pos_blocks) % tiles_per_chunk))
    vst_spec = pl.BlockSpec(
        (None, MOBA_HEADS, None, MOBA_BLOCK, LANES),
        lambda i: (i // pos_blocks, 0, (i % pos_blocks) // 2, 0, (i % pos_blocks) % 2))
    return pl.pallas_call(
        _proj_even_kernel,
        grid=(n // tm,),
        in_specs=[
            pl.BlockSpec((tm, D_MODEL), lambda i: (i, 0)),
            pl.BlockSpec((1, D_MODEL), lambda i: (0, 0)),
            pl.BlockSpec((D_MODEL, EVEN_IN_WIDTH), lambda i: (0, 0)),
            tab, tab, tab, tab,
        ],
        out_specs=[
            pl.BlockSpec((tm, EVEN_QKV_WIDTH), lambda i: (i, 0)),
            pl.BlockSpec((tm, RET_V_WIDTH), lambda i: (i, 0)),
            pl.BlockSpec((1, tm // MOBA_BLOCK, MOBA_WIDTH), lambda i: (i, 0, 0)),
            kst_spec, vst_spec,
        ],
        out_shape=[
            jax.ShapeDtypeStruct((n, EVEN_QKV_WIDTH), BF16),
            jax.ShapeDtypeStruct((n, RET_V_WIDTH), F32),
            jax.ShapeDtypeStruct((n // tm, tm // MOBA_BLOCK, MOBA_WIDTH), F32),
            jax.ShapeDtypeStruct((batch, MOBA_HEADS, nblk // MOBA_CHUNK, MOBA_BLOCK,
                                  MOBA_CHUNK * HEAD_DIM), BF16),
            jax.ShapeDtypeStruct((batch, MOBA_HEADS, nblk // 4, MOBA_BLOCK, 4 * HEAD_DIM), BF16),
        ],
        compiler_params=pltpu.CompilerParams(
            dimension_semantics=("arbitrary",), vmem_limit_bytes=VMEM_LIMIT),
        name="proj_even",
    )(x2, gain.reshape(1, D_MODEL), w_bf, rc, rs, tc, ts)


def _moba_kernel(q_ref, ko_ref, vo_ref, km_ref, kst_ref, vst_ref, o_ref):
    i = pl.program_id(2)
    nblk = km_ref.shape[0]
    bq = MOBA_BLOCK
    kw = MOBA_CHUNK * HEAD_DIM
    n_chunks = (i + MOBA_CHUNK - 1) // MOBA_CHUNK
    blk = lax.broadcasted_iota(jnp.int32, (bq, nblk), 1)
    blk_f = blk.astype(F32)
    row = lax.broadcasted_iota(jnp.int32, (bq, bq), 0)
    col = lax.broadcasted_iota(jnp.int32, (bq, bq), 1)
    lane_kw = lax.broadcasted_iota(jnp.int32, (1, kw), 1)
    lane_blk = (lane_kw // HEAD_DIM).astype(F32)
    lane_grp = (lane_kw // MOBA_BLOCK).astype(F32)
    lane_sub = lax.broadcasted_iota(jnp.int32, (1, bq), 1) // HEAD_DIM
    outs = []
    for h in range(2):
        sl = slice(h * HEAD_DIM, (h + 1) * HEAD_DIM)
        q = q_ref[:, sl]
        km = km_ref[:, sl]
        km_hi = km.astype(BF16)
        km_lo = (km - km_hi.astype(F32)).astype(BF16)
        gate = _dot_nt(q, km_hi) + _dot_nt(q, km_lo)
        g = jnp.where(blk < i, gate, NEG_INF)
        sels = []
        for _ in range(MOBA_TOPK):
            m = jnp.max(g, axis=-1, keepdims=True)
            cand = (g == m) & (m > NEG_INF)
            idx = jnp.min(jnp.where(cand, blk_f, float(nblk)), axis=-1, keepdims=True)
            g = jnp.where(blk_f == idx, NEG_INF, g)
            sels.append(jnp.where(idx < nblk, idx, -1.0))
        sel_f = jnp.concatenate(sels, axis=0)
        sel = sel_f.astype(jnp.int32)

        s_own = jnp.where(col <= row, _dot_nt(q, ko_ref[:, sl]), NEG_INF)

        q2 = jnp.concatenate([q, q], axis=1)
        qt = jnp.tile(jnp.concatenate([q2] * MOBA_TOPK, axis=0), (1, kw // LANES))
        sel_blk = sel_f.astype(BF16)

        def scores(c, acc, h=h, qt=qt, sel_blk=sel_blk):
            ids = (lane_blk + (c * MOBA_CHUNK).astype(F32)).astype(BF16)
            lhs = jnp.where(sel_blk == ids, qt, jnp.zeros_like(qt))
            return acc + _dot_nt(lhs, kst_ref[h, c])

        s_sel = lax.fori_loop(0, n_chunks, scores, jnp.zeros((MOBA_TOPK * bq, bq), F32))
        s_sel = jnp.where(sel >= 0, s_sel, NEG_INF)
        m = jnp.max(s_own, axis=-1, keepdims=True)
        for r in range(MOBA_TOPK):
            m = jnp.maximum(m, jnp.max(s_sel[r * bq:(r + 1) * bq], axis=-1, keepdims=True))
        p_own = jnp.exp(s_own - m)
        p_sel = jnp.exp(s_sel - jnp.concatenate([m] * MOBA_TOPK, axis=0))
        den = jnp.sum(p_own, axis=-1, keepdims=True)
        for r in range(MOBA_TOPK):
            den = den + jnp.sum(p_sel[r * bq:(r + 1) * bq], axis=-1, keepdims=True)

        pt = jnp.tile(p_sel.astype(BF16), (1, kw // bq))
        sel_grp = lax.shift_right_arithmetic(sel, 2).astype(F32).astype(BF16)

        def values(c, acc, h=h, pt=pt, sel_grp=sel_grp):
            ids = (lane_grp + (c * (MOBA_CHUNK // 4)).astype(F32)).astype(BF16)
            lhs = jnp.where(sel_grp == ids, pt, jnp.zeros_like(pt))
            return acc + _dot(lhs, vst_ref[h, c])

        o4 = lax.fori_loop(0, n_chunks, values, jnp.zeros((MOBA_TOPK * bq, bq), F32))
        o4 = jnp.where(lane_sub == (sel & 3), o4, 0.0)
        x = o4[:bq]
        for r in range(1, MOBA_TOPK):
            x = x + o4[r * bq:(r + 1) * bq]
        y = x[:, :LANES] + x[:, LANES:]
        y = y + pltpu.roll(y, HEAD_DIM, 1)
        outs.append((_dot(p_own.astype(BF16), vo_ref[:, sl]) + y[:, :HEAD_DIM]) / den)
    o_ref[...] = jnp.concatenate(outs, axis=-1).astype(BF16)


def _moba(qkv3, kmean3, kst, vst):
    b, s, _ = qkv3.shape
    nblk = s // MOBA_BLOCK
    pairs = MOBA_HEADS // 2
    n_chunks = nblk // MOBA_CHUNK
    kw = MOBA_CHUNK * HEAD_DIM
    own = lambda t: pl.BlockSpec((None, MOBA_BLOCK, LANES), lambda bi, hp, i: (bi, i, t * pairs + hp))
    return pl.pallas_call(
        _moba_kernel,
        grid=(b, pairs, nblk),
        in_specs=[
            own(0), own(1), own(2),
            pl.BlockSpec((None, nblk, LANES), lambda bi, hp, i: (bi, 0, hp)),
            pl.BlockSpec((None, 2, n_chunks, MOBA_BLOCK, kw), lambda bi, hp, i: (bi, hp, 0, 0, 0)),
            pl.BlockSpec((None, 2, n_chunks, kw, MOBA_BLOCK), lambda bi, hp, i: (bi, hp, 0, 0, 0)),
        ],
        out_specs=own(0),
        out_shape=jax.ShapeDtypeStruct((b, s, MOBA_WIDTH), BF16),
        compiler_params=pltpu.CompilerParams(
            dimension_semantics=("arbitrary", "arbitrary", "arbitrary"),
            vmem_limit_bytes=VMEM_LIMIT),
        name="moba",
    )(qkv3, qkv3, qkv3, kmean3, kst, vst.reshape(b, MOBA_HEADS, n_chunks, kw, MOBA_BLOCK))


def _retention_kernel(q_ref, k_ref, v_ref, g_ref, dec_ref, tail_ref, head_ref, gc_ref,
                      o_ref, state_ref):
    @pl.when(pl.program_id(2) == 0)
    def _():
        state_ref[...] = jnp.zeros_like(state_ref)

    c = RET_CHUNK
    for h in range(2):
        qk = slice(h * RET_QK_DIM, (h + 1) * RET_QK_DIM)
        vv = slice(h * RET_V_DIM, (h + 1) * RET_V_DIM)
        dec, tail, head, gc = dec_ref[h], tail_ref[h], head_ref[h], gc_ref[h]
        for u in range(q_ref.shape[0] // c):
            rows = slice(u * c, (u + 1) * c)
            q, k, v = q_ref[rows, qk], k_ref[rows, qk], v_ref[rows, vv]
            scores = _dot_nt(q, k) * dec
            state = state_ref[h]
            y = (_dot(scores.astype(BF16), v)
                 + _dot((q.astype(F32) * head).astype(BF16), state.astype(BF16)))
            state_ref[h] = gc * state + _dot_tn((k.astype(F32) * tail).astype(BF16), v)
            mu = jnp.mean(y, axis=-1, keepdims=True)
            yc = y - mu
            var = jnp.mean(yc * yc, axis=-1, keepdims=True)
            gate = g_ref[rows, vv]
            swish = gate / (1.0 + jnp.exp(-gate))
            o_ref[rows, vv] = (yc * lax.rsqrt(var + NORM_EPS) * swish).astype(BF16)


def _retention(qkv3, rg3):
    b, s, _ = qkv3.shape
    t = RET_ROWS
    pairs = RET_HEADS // 2
    dec, tail, head, gc = _retention_tables()
    q0 = 3 * MOBA_WIDTH // LANES
    k0 = q0 + RET_QK_WIDTH // LANES
    v0 = (3 * MOBA_WIDTH + 2 * RET_QK_WIDTH) // (2 * RET_V_DIM)
    return pl.pallas_call(
        _retention_kernel,
        grid=(b, pairs, s // t),
        in_specs=[
            pl.BlockSpec((None, t, LANES), lambda bi, hp, n: (bi, n, q0 + hp)),
            pl.BlockSpec((None, t, LANES), lambda bi, hp, n: (bi, n, k0 + hp)),
            pl.BlockSpec((None, t, 2 * RET_V_DIM), lambda bi, hp, n: (bi, n, v0 + hp)),
            pl.BlockSpec((None, t, 2 * RET_V_DIM), lambda bi, hp, n: (bi, n, hp)),
            pl.BlockSpec((2, RET_CHUNK, RET_CHUNK), lambda bi, hp, n: (hp, 0, 0)),
            pl.BlockSpec((2, RET_CHUNK, 1), lambda bi, hp, n: (hp, 0, 0)),
            pl.BlockSpec((2, RET_CHUNK, 1), lambda bi, hp, n: (hp, 0, 0)),
            pl.BlockSpec((2, 1, 1), lambda bi, hp, n: (hp, 0, 0)),
        ],
        out_specs=pl.BlockSpec((None, t, 2 * RET_V_DIM), lambda bi, hp, n: (bi, n, hp)),
        out_shape=jax.ShapeDtypeStruct((b, s, RET_V_WIDTH), BF16),
        scratch_shapes=[pltpu.VMEM((2, RET_QK_DIM, RET_V_DIM), F32)],
        compiler_params=pltpu.CompilerParams(
            dimension_semantics=("arbitrary", "arbitrary", "arbitrary"),
            vmem_limit_bytes=VMEM_LIMIT),
        name="retention",
    )(qkv3, qkv3, qkv3, rg3, dec, tail, head, gc)


def _route(xn, wr_hi_ref, wr_lo_ref, br_ref):
    x_hi = xn.astype(BF16)
    x_lo = (xn - x_hi.astype(F32)).astype(BF16)
    w_hi = wr_hi_ref[...]
    z = _dot(x_hi, w_hi) + _dot(x_lo, w_hi) + _dot(x_hi, wr_lo_ref[...]) + br_ref[...]
    lane = lax.broadcasted_iota(jnp.int32, z.shape, 1)

    def top1(v):
        m = jnp.max(v, axis=-1, keepdims=True)
        return m, jnp.min(jnp.where(v == m, lane, ROUTER_WIDTH), axis=-1, keepdims=True)

    g_logit = jnp.where(lane < MOE_GROUPS, z, NEG_INF)
    g_max, grp = top1(g_logit)
    g_val = 1.0 / jnp.sum(jnp.exp(g_logit - g_max), axis=-1, keepdims=True)
    lo = MOE_GROUPS + MOE_EXPERTS_PER_GROUP * grp
    f = jnp.where((lane >= lo) & (lane < lo + MOE_EXPERTS_PER_GROUP), z, NEG_INF)
    v1, i1 = top1(f)
    v2, i2 = top1(jnp.where(lane == i1, NEG_INF, f))
    t = jnp.exp(v2 - v1)
    g1 = g_val / (1.0 + t)
    g2 = g_val * t / (1.0 + t)
    e1 = (i1 - MOE_GROUPS).astype(F32)
    e2 = (i2 - MOE_GROUPS).astype(F32)
    return jnp.where(lane == 0, e1, jnp.where(lane == 1, e2, jnp.where(
        lane == 2, g1, jnp.where(lane == 3, g2, 0.0))))


def _outproj_even_kernel(x_ref, a_ref, r_ref, wo_ref, gn_ref, wr_hi_ref, wr_lo_ref, br_ref,
                         h_ref, xn_ref, rt_ref):
    h = x_ref[...] + (_dot(a_ref[...], wo_ref[:MOBA_WIDTH, :])
                      + _dot(r_ref[...], wo_ref[MOBA_WIDTH:, :]))
    h_ref[...] = h
    xn = _rms(h, gn_ref[...])
    xn_ref[...] = xn
    rt_ref[...] = _route(xn, wr_hi_ref, wr_lo_ref, br_ref)


def _outproj_even(x2, a2, r2, wo_bf, gain, router):
    n = x2.shape[0]
    tm = PROJ_ROWS
    row = lambda w: pl.BlockSpec((tm, w), lambda i: (i, 0))
    full = lambda r, w: pl.BlockSpec((r, w), lambda i: (0, 0))
    return pl.pallas_call(
        _outproj_even_kernel,
        grid=(n // tm,),
        in_specs=[row(D_MODEL), row(MOBA_WIDTH), row(RET_V_WIDTH), full(D_MODEL, D_MODEL),
                  full(1, D_MODEL), full(D_MODEL, ROUTER_WIDTH), full(D_MODEL, ROUTER_WIDTH),
                  full(1, ROUTER_WIDTH)],
        out_specs=[row(D_MODEL), row(D_MODEL), row(ROUTER_WIDTH)],
        out_shape=[jax.ShapeDtypeStruct((n, D_MODEL), F32),
                   jax.ShapeDtypeStruct((n, D_MODEL), F32),
                   jax.ShapeDtypeStruct((n, ROUTER_WIDTH), F32)],
        compiler_params=pltpu.CompilerParams(
            dimension_semantics=("arbitrary",), vmem_limit_bytes=VMEM_LIMIT),
        name="outproj_even",
    )(x2, a2, r2, wo_bf, gain.reshape(1, D_MODEL), *router)


def _outproj_odd_kernel(h_ref, o1_ref, o2_ref, o3_ref, l1_ref, l2_ref, l3_ref, wo_ref,
                        gn_ref, wr_hi_ref, wr_lo_ref, br_ref, hout_ref, xn_ref, rt_ref):
    pieces = []
    for hp in range(DIL_HEADS // 2):
        la, lb, lc = l1_ref[hp], l2_ref[hp], l3_ref[hp]
        m = jnp.maximum(jnp.maximum(la, lb), lc)
        ea, eb, ec = jnp.exp(la - m), jnp.exp(lb - m), jnp.exp(lc - m)
        mix = (ea * o1_ref[hp] + eb * o2_ref[hp] + ec * o3_ref[hp]) / (ea + eb + ec)
        pieces.append(mix.astype(BF16))
    h = h_ref[...] + _dot(jnp.concatenate(pieces, axis=-1), wo_ref[...])
    hout_ref[...] = h
    xn = _rms(h, gn_ref[...])
    xn_ref[...] = xn
    rt_ref[...] = _route(xn, wr_hi_ref, wr_lo_ref, br_ref)


def _outproj_odd(h2, branches, wo_bf, gain, router, seq):
    n = h2.shape[0]
    tm = PROJ_ROWS
    pairs = DIL_HEADS // 2
    per_seq = seq // tm
    row = lambda w: pl.BlockSpec((tm, w), lambda i: (i, 0))
    full = lambda r, w: pl.BlockSpec((r, w), lambda i: (0, 0))
    hm = pl.BlockSpec((None, pairs, tm, LANES), lambda i: (i // per_seq, 0, i % per_seq, 0))
    outs = [o for (o, _) in branches]
    lses = [l for (_, l) in branches]
    return pl.pallas_call(
        _outproj_odd_kernel,
        grid=(n // tm,),
        in_specs=[row(D_MODEL), hm, hm, hm, hm, hm, hm, full(D_MODEL, D_MODEL),
                  full(1, D_MODEL), full(D_MODEL, ROUTER_WIDTH), full(D_MODEL, ROUTER_WIDTH),
                  full(1, ROUTER_WIDTH)],
        out_specs=[row(D_MODEL), row(D_MODEL), row(ROUTER_WIDTH)],
        out_shape=[jax.ShapeDtypeStruct((n, D_MODEL), F32),
                   jax.ShapeDtypeStruct((n, D_MODEL), F32),
                   jax.ShapeDtypeStruct((n, ROUTER_WIDTH), F32)],
        compiler_params=pltpu.CompilerParams(
            dimension_semantics=("arbitrary",), vmem_limit_bytes=VMEM_LIMIT),
        name="outproj_odd",
    )(h2, *outs, *lses, wo_bf, gain.reshape(1, D_MODEL), *router)


def _plan_rows(expert):
    rb = EXPERT_ROWS
    n_assign = expert.size
    e_flat = expert.T.reshape(-1)
    order = jnp.argsort(e_flat).astype(jnp.int32)
    experts = jnp.arange(MOE_EXPERTS, dtype=jnp.int32)
    counts = jnp.sum((e_flat[:, None] == experts[None, :]).astype(jnp.int32), axis=0)
    starts = jnp.cumsum(counts) - counts
    padded = ((counts + rb - 1) // rb) * rb
    p_ends = jnp.cumsum(padded)
    p_starts = p_ends - padded
    n_blocks = n_assign // rb + MOE_EXPERTS
    blk_start = jnp.arange(n_blocks, dtype=jnp.int32) * rb
    blk_exp = jnp.minimum(jnp.sum(p_ends[None, :] <= blk_start[:, None], axis=1),
                          MOE_EXPERTS - 1).astype(jnp.int32)
    blk_offset = blk_start - p_starts[blk_exp]
    blk_count = counts[blk_exp]
    blk_valid = jnp.clip(blk_count - blk_offset, 0, rb).astype(jnp.int32)
    within = blk_offset[:, None] + jnp.arange(rb, dtype=jnp.int32)[None, :]
    src = jnp.clip(starts[blk_exp][:, None] + within, 0, n_assign - 1)
    row_dst = jnp.where(within < blk_count[:, None], order[src], 0).reshape(-1)
    n_used = (p_ends[-1] // rb).astype(jnp.int32).reshape(1)
    return blk_exp, blk_valid, row_dst, n_used


def _expert_kernel(blk_exp, blk_valid, row_dst, n_used,
                   wg_ref, wu_ref, wd_ref, x_hbm, out_hbm,
                   xbuf, ybuf, gsem, ssem):
    del blk_exp
    rb = EXPERT_ROWS
    n_tok = x_hbm.shape[0]
    i = pl.program_id(0)
    used = n_used[0]
    slot = i % 2

    def gather_copy(tok, r, s):
        return pltpu.make_async_copy(x_hbm.at[pl.ds(tok, 1)], xbuf.at[s, pl.ds(r, 1)], gsem.at[s])

    def scatter_copy(dst, r, s):
        return pltpu.make_async_copy(ybuf.at[s, pl.ds(r, 1)], out_hbm.at[pl.ds(dst, 1)], ssem.at[s])

    def start_gather(b, s):
        def body(r, carry):
            a = row_dst[b * rb + r]
            gather_copy(a - jnp.where(a >= n_tok, n_tok, 0), r, s).start()
            return carry
        lax.fori_loop(0, rb, body, 0, unroll=8)

    def wait_scatter(count, s):
        wide = 16

        def wide_body(r, carry):
            pltpu.make_async_copy(ybuf.at[s, pl.ds(0, wide)], out_hbm.at[pl.ds(0, wide)],
                                  ssem.at[s]).wait()
            return carry
        lax.fori_loop(0, count // wide, wide_body, 0)

        def body(r, carry):
            scatter_copy(0, 0, s).wait()
            return carry
        lax.fori_loop(0, count % wide, body, 0)

    @pl.when(i == 0)
    def _():
        start_gather(0, 0)

    @pl.when(i < used)
    def _():
        @pl.when(i + 1 < used)
        def _():
            start_gather(i + 1, 1 - slot)

        pltpu.make_async_copy(x_hbm.at[pl.ds(0, rb)], xbuf.at[slot], gsem.at[slot]).wait()

        @pl.when(i >= 2)
        def _():
            wait_scatter(blk_valid[i - 2], slot)

        xb = xbuf[slot].astype(BF16)
        g = _dot(xb, wg_ref[...])
        u = _dot(xb, wu_ref[...])
        hid = (g / (1.0 + jnp.exp(-g)) * u).astype(BF16)
        ybuf[slot] = _dot(hid, wd_ref[...])

        def scatter_body(r, carry):
            scatter_copy(row_dst[i * rb + r], r, slot).start()
            return carry
        def scatter_body8(r8, carry):
            for j in range(8):
                scatter_body(r8 * 8 + j, carry)
            return carry
        n_valid = blk_valid[i]
        lax.fori_loop(0, n_valid // 8, scatter_body8, 0)
        lax.fori_loop((n_valid // 8) * 8, n_valid, scatter_body, 0)

        @pl.when(i == used - 1)
        def _():
            wait_scatter(blk_valid[i], slot)

            @pl.when(i >= 1)
            def _():
                wait_scatter(blk_valid[i - 1], 1 - slot)


def _experts(xn, plan, wg_bf, wu_bf, wd_bf):
    blk_exp, blk_valid, row_dst, n_used = plan
    n = xn.shape[0]
    rb = EXPERT_ROWS
    n_blocks = blk_exp.shape[0]
    w_in = pl.BlockSpec((None, D_MODEL, MOE_HIDDEN), lambda i, be, bv, rd, nu: (be[i], 0, 0))
    w_out = pl.BlockSpec((None, MOE_HIDDEN, D_MODEL), lambda i, be, bv, rd, nu: (be[i], 0, 0))
    return pl.pallas_call(
        _expert_kernel,
        grid_spec=pltpu.PrefetchScalarGridSpec(
            num_scalar_prefetch=4,
            grid=(n_blocks,),
            in_specs=[
                w_in, w_in, w_out,
                pl.BlockSpec(memory_space=pl.ANY),
            ],
            out_specs=pl.BlockSpec(memory_space=pl.ANY),
            scratch_shapes=[
                pltpu.VMEM((2, rb, D_MODEL), F32),
                pltpu.VMEM((2, rb, D_MODEL), F32),
                pltpu.SemaphoreType.DMA((2,)),
                pltpu.SemaphoreType.DMA((2,)),
            ],
        ),
        out_shape=jax.ShapeDtypeStruct((MOE_TOPK * n, D_MODEL), F32),
        compiler_params=pltpu.CompilerParams(
            dimension_semantics=("arbitrary",), vmem_limit_bytes=VMEM_LIMIT),
        name="experts",
    )(blk_exp, blk_valid, row_dst, n_used, wg_bf, wu_bf, wd_bf, xn)


def _moe(xn, route, wg_bf, wu_bf, wd_bf):
    expert = route[:, :MOE_TOPK].astype(jnp.int32)
    return _experts(xn, _plan_rows(expert), wg_bf, wu_bf, wd_bf)


def _moe_combine(h_ref, y0_ref, y1_ref, rt_ref):
    rt = rt_ref[...]
    return h_ref[...] + (y0_ref[...] * rt[:, 2:3] + y1_ref[...] * rt[:, 3:4])


def _moe_output_specs(n, tm):
    return [pl.BlockSpec((tm, D_MODEL), lambda i: (i, 0)),
            pl.BlockSpec((tm, D_MODEL), lambda i: (n // tm + i, 0))]


def _proj_odd_kernel(h_ref, y0_ref, y1_ref, rt_ref, g_ref, w_ref, rc_ref, rs_ref,
                     hout_ref, q_ref, k_ref, v_ref):
    tm = h_ref.shape[0]
    h = _moe_combine(h_ref, y0_ref, y1_ref, rt_ref)
    hout_ref[...] = h
    xn = _rms(h, g_ref[...]).astype(BF16)
    lane = lax.broadcasted_iota(jnp.int32, (tm, LANES), 1)
    first_half = (lane % HEAD_DIM) < (HEAD_DIM // 2)
    rc, rs = rc_ref[...], rs_ref[...]
    scale = HEAD_DIM ** -0.5
    n_chunk = 4
    pairs = DIL_HEADS // 2
    for j in range(ODD_IN_WIDTH // (n_chunk * LANES)):
        acc = _dot(xn, w_ref[:, j * n_chunk * LANES:(j + 1) * n_chunk * LANES])
        for u in range(n_chunk):
            c = j * n_chunk + u
            a = acc[:, u * LANES:(u + 1) * LANES]
            if c < pairs:
                q_ref[c] = ((a * rc + _swap_half(a, first_half) * rs) * scale).astype(BF16)
            elif c < 2 * pairs:
                k_ref[c - pairs] = (a * rc + _swap_half(a, first_half) * rs).astype(BF16)
            else:
                v_ref[c - 2 * pairs] = a.astype(BF16)


def _proj_odd(h1, y2, route, gain, w_bf, batch, seq):
    n = h1.shape[0]
    tm = PROJ_ROWS
    pairs = DIL_HEADS // 2
    rc, rs = _rotary_tables(seq)
    per_seq = seq // tm
    tab = pl.BlockSpec((tm, LANES), lambda i: (i % per_seq, 0))
    hm = pl.BlockSpec((None, pairs, tm, LANES), lambda i: (i // per_seq, 0, i % per_seq, 0))
    hm_shape = jax.ShapeDtypeStruct((batch, pairs, seq, LANES), BF16)
    return pl.pallas_call(
        _proj_odd_kernel,
        grid=(n // tm,),
        in_specs=[
            pl.BlockSpec((tm, D_MODEL), lambda i: (i, 0)),
            *_moe_output_specs(n, tm),
            pl.BlockSpec((tm, ROUTER_WIDTH), lambda i: (i, 0)),
            pl.BlockSpec((1, D_MODEL), lambda i: (0, 0)),
            pl.BlockSpec((D_MODEL, ODD_IN_WIDTH), lambda i: (0, 0)),
            tab, tab,
        ],
        out_specs=[pl.BlockSpec((tm, D_MODEL), lambda i: (i, 0)), hm, hm, hm],
        out_shape=[jax.ShapeDtypeStruct((n, D_MODEL), F32), hm_shape, hm_shape, hm_shape],
        compiler_params=pltpu.CompilerParams(
            dimension_semantics=("arbitrary",), vmem_limit_bytes=VMEM_LIMIT),
        name="proj_odd",
    )(h1, y2, y2, route, gain.reshape(1, D_MODEL), w_bf, rc, rs)


def _dilated_kernel(q_ref, kp_ref, ko_ref, vp_ref, vo_ref, o_ref, l_ref):
    n = pl.program_id(3)
    blk = DIL_BLOCK
    a = lax.broadcasted_iota(jnp.int32, (blk, 2 * blk), 0)
    e = lax.broadcasted_iota(jnp.int32, (blk, 2 * blk), 1)
    dist = blk + a - e
    band = (dist >= 0) & (dist <= blk)
    first = band & ((n > 0) | (e >= blk))
    for u in range(q_ref.shape[0] // blk):
        rows = slice(u * blk, (u + 1) * blk)
        o_parts, l_parts = [], []
        for h in range(2):
            sl = slice(h * HEAD_DIM, (h + 1) * HEAD_DIM)
            q = q_ref[rows, sl]
            if u == 0:
                k = jnp.concatenate([kp_ref[:, sl], ko_ref[:blk, sl]], axis=0)
                v = jnp.concatenate([vp_ref[:, sl], vo_ref[:blk, sl]], axis=0)
                mask = first
            else:
                k = ko_ref[(u - 1) * blk:(u + 1) * blk, sl]
                v = vo_ref[(u - 1) * blk:(u + 1) * blk, sl]
                mask = band
            s = jnp.where(mask, _dot_nt(q, k), NEG_INF)
            m = jnp.max(s, axis=-1, keepdims=True)
            p = jnp.exp(s - m)
            den = jnp.sum(p, axis=-1, keepdims=True)
            o_parts.append(_dot((p / den).astype(BF16), v))
            l_parts.append(jnp.broadcast_to(m + jnp.log(den), (blk, HEAD_DIM)))
        o_ref[rows, :] = jnp.concatenate(o_parts, axis=-1).astype(o_ref.dtype)
        l_ref[rows, :] = jnp.concatenate(l_parts, axis=-1)


def _dilated_branch(q_hm, k_hm, v_hm, dil):
    b, pairs, s, _ = q_hm.shape
    length = s // dil
    tq = min(DIL_ROWS, length)
    sub = tq // DIL_BLOCK
    view = lambda t: t.reshape(b, pairs, length, dil * LANES)
    own = pl.BlockSpec((None, None, tq, LANES), lambda bi, hp, r, n: (bi, hp, n, r))
    prev = pl.BlockSpec((None, None, DIL_BLOCK, LANES),
                        lambda bi, hp, r, n: (bi, hp, jnp.maximum(n * sub - 1, 0), r))
    shape = lambda dt: jax.ShapeDtypeStruct((b, pairs, length, dil * LANES), dt)
    o, lse = pl.pallas_call(
        _dilated_kernel,
        grid=(b, pairs, dil, length // tq),
        in_specs=[own, prev, own, prev, own],
        out_specs=[own, own],
        out_shape=[shape(BF16), shape(F32)],
        compiler_params=pltpu.CompilerParams(
            dimension_semantics=("arbitrary",) * 4, vmem_limit_bytes=VMEM_LIMIT),
        name=f"dilated_{dil}",
    )(view(q_hm), view(k_hm), view(k_hm), view(v_hm), view(v_hm))
    return o.reshape(b, pairs, s, LANES), lse.reshape(b, pairs, s, LANES)


def _final_kernel(h_ref, y0_ref, y1_ref, rt_ref, g_ref, o_ref):
    o_ref[...] = _rms(_moe_combine(h_ref, y0_ref, y1_ref, rt_ref), g_ref[...])


def _final(h, y2, route, gain):
    n = h.shape[0]
    tm = PROJ_ROWS
    return pl.pallas_call(
        _final_kernel,
        grid=(n // tm,),
        in_specs=[pl.BlockSpec((tm, D_MODEL), lambda i: (i, 0)),
                  *_moe_output_specs(n, tm),
                  pl.BlockSpec((tm, ROUTER_WIDTH), lambda i: (i, 0)),
                  pl.BlockSpec((1, D_MODEL), lambda i: (0, 0))],
        out_specs=pl.BlockSpec((tm, D_MODEL), lambda i: (i, 0)),
        out_shape=jax.ShapeDtypeStruct((n, D_MODEL), F32),
        compiler_params=pltpu.CompilerParams(
            dimension_semantics=("arbitrary",), vmem_limit_bytes=VMEM_LIMIT),
        name="final_norm",
    )(h, y2, y2, route, gain.reshape(1, D_MODEL))


def _router_params(w_group, w_fine, b_group, b_fine):
    w = jnp.concatenate([w_group, w_fine], axis=-1).astype(F32)
    pad = ROUTER_WIDTH - w.shape[1]
    w = jnp.pad(w, ((0, 0), (0, pad)))
    bias = jnp.pad(jnp.concatenate([b_group, b_fine]).astype(F32), (0, pad)).reshape(1, ROUTER_WIDTH)
    w_hi = w.astype(BF16)
    return w_hi, (w - w_hi.astype(F32)).astype(BF16), bias


def kernel(x, mix_norm_even, w_in_even, w_out_even, mix_norm_odd, w_in_odd, w_out_odd, ffn_norm,
           w_router_group, b_router_group, w_router_expert, b_router_expert,
           w_expert_gate, w_expert_up, w_expert_down, final_norm):
    batch, seq, d = x.shape
    n = batch * seq
    x2 = x.reshape(n, d)

    qkv, rg, kmean, kst, vst = _proj_even(x2, mix_norm_even[0], w_in_even[0].astype(BF16),
                                          batch, seq)
    qkv3 = qkv.reshape(batch, seq, EVEN_QKV_WIDTH)
    a_out = _moba(qkv3, kmean.reshape(batch, seq // MOBA_BLOCK, MOBA_WIDTH), kst, vst)
    r_out = _retention(qkv3, rg.reshape(batch, seq, RET_V_WIDTH))
    router = _router_params(w_router_group[0], w_router_expert[0],
                            b_router_group[0], b_router_expert[0])
    h1, xn, route = _outproj_even(x2, a_out.reshape(n, MOBA_WIDTH), r_out.reshape(n, RET_V_WIDTH),
                                  w_out_even[0].astype(BF16), ffn_norm[0], router)
    y = _moe(xn, route, w_expert_gate[0].astype(BF16), w_expert_up[0].astype(BF16),
             w_expert_down[0].astype(BF16))

    h2, q_hm, k_hm, v_hm = _proj_odd(h1, y, route, mix_norm_odd[0], w_in_odd[0].astype(BF16),
                                     batch, seq)
    branches = [_dilated_branch(q_hm, k_hm, v_hm, dil) for (_, dil) in DIL_PATTERNS]
    router = _router_params(w_router_group[1], w_router_expert[1],
                            b_router_group[1], b_router_expert[1])
    h3, xn, route = _outproj_odd(h2, branches, w_out_odd[0].astype(BF16), ffn_norm[1], router, seq)
    y = _moe(xn, route, w_expert_gate[1].astype(BF16), w_expert_up[1].astype(BF16),
             w_expert_down[1].astype(BF16))
    return _final(h3, y, route, final_norm).reshape(batch, seq, d)
```
